```python
import math
import jax
import jax.numpy as jnp
from jax import lax
import numpy as np

D_MODEL = 1024
BATCH = 16
SEQ = 256
DEPTH = 4
DEC_BATCH = 8
DEC_SEQ = 2048
PAST_LEN = 512

GRID_W = 64
N_MIXERS = 3
N_GMLP_LAYERS = (DEPTH + 2) // 3
N_SSD_LAYERS = (DEPTH + 1) // 3
N_HGRN_LAYERS = DEPTH // 3

ALPHA = (2.0 * DEPTH) ** 0.25
BETA = (8.0 * DEPTH) ** -0.25
LN_EPS = 1e-5

GMLP_CHUNK = 128
GMLP_WIDTH = 2 * D_MODEL
GMLP_GROUPS = 8
GMLP_GROUP_DIM = GMLP_WIDTH // GMLP_GROUPS

SSD_INNER = 2 * D_MODEL
SSD_HEADDIM = 64
SSD_HEADS = SSD_INNER // SSD_HEADDIM
SSD_GROUPS = 4
SSD_STATE = 128
SSD_CONV = 3
SSD_CHUNK = 128
SSD_CONV_DIM = SSD_INNER + 2 * SSD_GROUPS * SSD_STATE
SSD_IN_DIM = SSD_INNER + SSD_CONV_DIM + 2 * SSD_HEADS

HGRN_HEADS = 8
HGRN_KDIM = 128
HGRN_VDIM = D_MODEL // HGRN_HEADS
HGRN_KW = HGRN_HEADS * HGRN_KDIM
HGRN_VW = HGRN_HEADS * HGRN_VDIM
HGRN_CHUNK = 64
HGRN_IN_DIM = 3 * HGRN_KW + 2 * HGRN_VW

N_EXPERTS = 16
N_EXPERT_GROUPS = 4
EXPERTS_PER_GROUP = N_EXPERTS // N_EXPERT_GROUPS
TOP_K = 2
D_EXPERT = 256

kernel_name = 'hybrid_gmlp_ssd_hgrn2_moe_diffusion_step'


def layer_norm(x, g, b):
    xf = x.astype(jnp.float32)
    mu = jnp.mean(xf, -1, keepdims=True)
    var = jnp.mean(jnp.square(xf - mu), -1, keepdims=True)
    return ((xf - mu) * lax.rsqrt(var + LN_EPS) * g + b).astype(x.dtype)


def rms_norm(x, g):
    xf = x.astype(jnp.float32)
    return (xf * lax.rsqrt(jnp.mean(xf * xf, -1, keepdims=True) + LN_EPS) * g).astype(x.dtype)


def grid_pos_embed(rows, dtype):
    quarter = D_MODEL // 4
    freq = jnp.exp(-math.log(10000.0) / quarter * jnp.arange(quarter, dtype=jnp.float32))
    r = jnp.repeat(jnp.arange(rows, dtype=jnp.float32), GRID_W)
    col = jnp.tile(jnp.arange(GRID_W, dtype=jnp.float32), rows)
    ar = r[:, None] * freq[None, :]
    ac = col[:, None] * freq[None, :]
    return jnp.concatenate([jnp.sin(ar), jnp.cos(ar), jnp.sin(ac), jnp.cos(ac)], axis=-1).astype(dtype)


def modulation(cond, w_mod, b_mod):
    m = jax.nn.silu(cond) @ w_mod + b_mod
    return jnp.split(m[:, None, :], 6, axis=-1)


def dwconv_centred(x, w, b):
    pad = w.shape[0] // 2
    y = lax.conv_general_dilated(x, w[:, None, :], window_strides=(1,), padding=[(pad, pad)],
                                 dimension_numbers=('NWC', 'WIO', 'NWC'),
                                 feature_group_count=x.shape[-1])
    return y + b


def chunk_gmlp(h, w_in, ln_g, ln_b, w_s, b_s, w_out):
    bsz, L, _ = h.shape
    z = jax.nn.gelu(h @ w_in, approximate=False)
    u, v = jnp.split(z, 2, axis=-1)
    v = layer_norm(v, ln_g, ln_b)
    v = v.reshape(bsz, L // GMLP_CHUNK, GMLP_CHUNK, GMLP_GROUPS, GMLP_GROUP_DIM)
    mixed = jnp.einsum('hqp,bnphd->bnqhd', w_s, v) + b_s.T[None, None, :, :, None]
    return (u * mixed.reshape(bsz, L, GMLP_WIDTH)) @ w_out


def ssd_scan(x, dt, A, Bm, Cm, h0):
    f32 = jnp.float32
    b, L, H, P = x.shape
    G, N = Bm.shape[2], Bm.shape[3]
    R = H // G
    Q = SSD_CHUNK
    nc = L // Q
    xc = x.astype(f32).reshape(b, nc, Q, G, R, P)
    Bc = Bm.astype(f32).reshape(b, nc, Q, G, N)
    Cc = Cm.astype(f32).reshape(b, nc, Q, G, N)
    dtc = dt.reshape(b, nc, Q, G, R)
    acs = jnp.cumsum(dtc * A.reshape(G, R), axis=2)
    mask = jnp.tril(jnp.ones((Q, Q), bool))
    seg = acs[:, :, :, None] - acs[:, :, None]
    decay = jnp.exp(jnp.where(mask[:, :, None, None], seg, -jnp.inf))
    cb = jnp.einsum('bclgk,bcsgk->bclsg', Cc, Bc)
    w = cb[..., None] * decay * dtc[:, :, None]
    y_diag = jnp.einsum('bclsgr,bcsgrp->bclgrp', w, xc)
    decay_end = jnp.exp(acs[:, :, -1:] - acs)
    states = jnp.einsum('bcsgk,bcsgr,bcsgrp->bcgrpk', Bc, decay_end * dtc, xc)
    chunk_decay = jnp.exp(acs[:, :, -1])

    def step(hc, inp):
        st, dec = inp
        return hc * dec[..., None, None] + st, hc

    h_last, h_in = lax.scan(step, h0.astype(f32).reshape(b, G, R, P, N),
                            (jnp.moveaxis(states, 1, 0), jnp.moveaxis(chunk_decay, 1, 0)))
    h_in = jnp.moveaxis(h_in, 0, 1)
    y_off = jnp.einsum('bclgk,bcgrpk,bclgr->bclgrp', Cc, h_in, jnp.exp(acs))
    return (y_diag + y_off).reshape(b, L, H, P), h_last.reshape(b, H, P, N)


def ssd_mixer(h, h0_f, h0_b, w_in, conv_w, conv_b, dt_bias, a_log, d_skip, norm_g, w_out):
    f32 = jnp.float32
    bsz, L, _ = h.shape
    proj = h @ w_in
    z, xbc, dt_raw = jnp.split(proj, [SSD_INNER, SSD_INNER + SSD_CONV_DIM], axis=-1)
    xbc = jax.nn.silu(dwconv_centred(xbc, conv_w, conv_b))
    xs, Bm, Cm = jnp.split(xbc, [SSD_INNER, SSD_INNER + SSD_GROUPS * SSD_STATE], axis=-1)
    xs = xs.reshape(bsz, L, SSD_HEADS, SSD_HEADDIM)
    Bm = Bm.reshape(bsz, L, SSD_GROUPS, SSD_STATE)
    Cm = Cm.reshape(bsz, L, SSD_GROUPS, SSD_STATE)
    dt = jax.nn.softplus(dt_raw.astype(f32).reshape(bsz, L, 2, SSD_HEADS) + dt_bias.astype(f32))
    A = -jnp.exp(a_log.astype(f32))
    flip = lambda t: jnp.flip(t, axis=1)
    y_f, s_f = ssd_scan(xs, dt[:, :, 0], A[0], Bm, Cm, h0_f)
    y_b, s_b = ssd_scan(flip(xs), flip(dt[:, :, 1]), A[1], flip(Bm), flip(Cm), h0_b)
    y = y_f + flip(y_b) + xs.astype(f32) * d_skip.astype(f32)[:, None]
    y = y.reshape(bsz, L, SSD_INNER) * jax.nn.silu(z.astype(f32))
    y = rms_norm(y, norm_g).astype(h.dtype)
    return y @ w_out, s_f, s_b


def hgrn_lower_bounds(lb_logits):
    p = jax.nn.softmax(lb_logits.astype(jnp.float32), axis=0)
    return jnp.cumsum(p, axis=0) - p[0]


def hgrn2_scan(q, k, v, logf, s0):
    b, L, H, K = q.shape
    Q = HGRN_CHUNK
    nc = L // Q
    to_chunks = lambda t: jnp.moveaxis(t.reshape(b, nc, Q, H, t.shape[-1]), 1, 0)
    mask = jnp.tril(jnp.ones((Q, Q), bool))

    def step(S, inp):
        qc, kc, vc, gc = inp
        bcs = jnp.cumsum(gc, axis=1)
        o_inter = jnp.einsum('bthk,bhkv->bthv', qc * jnp.exp(bcs), S)
        diff = bcs[:, :, None] - bcs[:, None]
        dec = jnp.exp(jnp.where(mask[None, :, :, None, None], diff, -jnp.inf))
        att = jnp.einsum('bthk,btshk,bshk->bhts', qc, dec, kc)
        o_intra = jnp.einsum('bhts,bshv->bthv', att, vc)
        last = bcs[:, -1]
        S_new = S * jnp.exp(last)[..., None] + jnp.einsum(
            'bshk,bshv->bhkv', kc * jnp.exp(last[:, None] - bcs), vc)
        return S_new, o_inter + o_intra

    s_last, o = lax.scan(step, s0, (to_chunks(q), to_chunks(k), to_chunks(v), to_chunks(logf)))
    return jnp.moveaxis(o, 0, 1).reshape(b, L, H, v.shape[-1]), s_last


def hgrn2_mixer(h, s0_f, s0_b, lb, w_in, norm_g, w_out):
    f32 = jnp.float32
    bsz, L, _ = h.shape
    proj = h @ w_in
    q, f_f, f_b, i, g = jnp.split(
        proj, [HGRN_KW, 2 * HGRN_KW, 3 * HGRN_KW, 3 * HGRN_KW + HGRN_VW], axis=-1)
    heads_k = lambda t: t.astype(f32).reshape(bsz, L, HGRN_HEADS, HGRN_KDIM)
    q = heads_k(q)
    v = i.astype(f32).reshape(bsz, L, HGRN_HEADS, HGRN_VDIM)
    lb = lb.reshape(2, HGRN_HEADS, HGRN_KDIM)
    f_fw = lb[0] + (1.0 - lb[0]) * jax.nn.sigmoid(heads_k(f_f))
    f_bw = lb[1] + (1.0 - lb[1]) * jax.nn.sigmoid(heads_k(f_b))
    flip = lambda t: jnp.flip(t, axis=1)
    o_f, s_f = hgrn2_scan(q, 1.0 - f_fw, v, jnp.log(f_fw), s0_f.astype(f32))
    o_b, s_b = hgrn2_scan(flip(q), flip(1.0 - f_bw), flip(v), flip(jnp.log(f_bw)), s0_b.astype(f32))
    o = rms_norm(o_f + flip(o_b), norm_g)
    o = o * jax.nn.silu(g.astype(f32).reshape(bsz, L, HGRN_HEADS, HGRN_VDIM))
    return o.reshape(bsz, L, HGRN_VW).astype(h.dtype) @ w_out, s_f, s_b


def moe(h, w_router, b_router, w_gu, w_down):
    scores = jax.nn.sigmoid((h @ w_router).astype(jnp.float32))
    sel = scores + b_router.astype(jnp.float32)
    sel_g = sel.reshape(sel.shape[:-1] + (N_EXPERT_GROUPS, EXPERTS_PER_GROUP))
    group_score = lax.top_k(sel_g, TOP_K)[0].sum(-1)
    best = jnp.argmax(group_score, axis=-1)
    in_group = best[..., None] == jnp.arange(N_EXPERT_GROUPS)
    masked = jnp.where(in_group[..., None], sel_g, -jnp.inf).reshape(sel.shape)
    _, idx = lax.top_k(masked, TOP_K)
    gsel = jnp.take_along_axis(scores, idx, axis=-1)
    gsel = gsel / jnp.sum(gsel, -1, keepdims=True)
    combine = jnp.sum(jax.nn.one_hot(idx, N_EXPERTS, dtype=jnp.float32) * gsel[..., None], axis=-2)

    def per_seq(args):
        xs, cw = args
        gate, up = jnp.split(jnp.einsum('ld,edf->lef', xs, w_gu), 2, axis=-1)
        hid = jax.nn.silu(gate) * up * cw[..., None].astype(xs.dtype)
        return jnp.einsum('lef,efd->ld', hid, w_down)

    return lax.map(per_seq, (h, combine))


def run_trunk(x, cond, ssd_init, hgrn_init, w_mod, b_mod, ln_g, ln_b,
              gmlp_w_in, gmlp_ln_g, gmlp_ln_b, gmlp_w_s, gmlp_b_s, gmlp_w_out,
              ssd_w_in, ssd_conv_w, ssd_conv_b, ssd_dt_bias, ssd_a_log, ssd_d_skip,
              ssd_norm_g, ssd_w_out, hgrn_w_in, hgrn_lb_logits, hgrn_norm_g, hgrn_w_out,
              moe_w_router, moe_b_router, moe_w_gu, moe_w_down):
    lower_bounds = hgrn_lower_bounds(hgrn_lb_logits)
    ssd_states, hgrn_states = [], []
    ia = ib = ic = 0
    for li in range(DEPTH):
        sh1, sc1, g1, sh2, sc2, g2 = modulation(cond, w_mod[li], b_mod[li])
        h = x * (1 + sc1) + sh1
        kind = li % N_MIXERS
        if kind == 0:
            out = chunk_gmlp(h, gmlp_w_in[ia], gmlp_ln_g[ia], gmlp_ln_b[ia], gmlp_w_s[ia],
                             gmlp_b_s[ia], gmlp_w_out[ia])
            ia += 1
        elif kind == 1:
            out, s_f, s_b = ssd_mixer(h, ssd_init[:, ib, 0], ssd_init[:, ib, 1], ssd_w_in[ib],
                                      ssd_conv_w[ib], ssd_conv_b[ib], ssd_dt_bias[ib],
                                      ssd_a_log[ib], ssd_d_skip[ib], ssd_norm_g[ib], ssd_w_out[ib])
            ssd_states.append(jnp.stack([s_f, s_b], axis=1))
            ib += 1
        else:
            out, s_f, s_b = hgrn2_mixer(h, hgrn_init[:, ic, 0], hgrn_init[:, ic, 1], lower_bounds[li],
                                        hgrn_w_in[ic], hgrn_norm_g[ic], hgrn_w_out[ic])
            hgrn_states.append(jnp.stack([s_f, s_b], axis=1))
            ic += 1
        x = layer_norm(ALPHA * x + g1 * out, ln_g[li, 0], ln_b[li, 0])
        h = x * (1 + sc2) + sh2
        x = layer_norm(ALPHA * x + g2 * moe(h, moe_w_router, moe_b_router, moe_w_gu[li], moe_w_down[li]),
                       ln_g[li, 1], ln_b[li, 1])
    return x, jnp.stack(ssd_states, axis=1).astype(x.dtype), jnp.stack(hgrn_states, axis=1).astype(x.dtype)


def setup_inputs(seed: int = 0) -> dict:
    key = jax.random.key(seed)
    ks = iter(jax.random.split(key, 40))
    d = D_MODEL

    def nrm(shape, scale):
        return jax.random.normal(next(ks), shape, jnp.float32) * scale

    x_prompt = nrm((BATCH, SEQ, d), 1.0)
    x_sample = nrm((DEC_BATCH, DEC_SEQ, d), 1.0)
    state_ssd = nrm((DEC_BATCH, N_SSD_LAYERS, 2, SSD_HEADS, SSD_HEADDIM, SSD_STATE), 1.0)
    state_hgrn = nrm((DEC_BATCH, N_HGRN_LAYERS, 2, HGRN_HEADS, HGRN_KDIM, HGRN_VDIM), 1.0)
    c = nrm((DEC_BATCH, d), 1.0)
    c_ctx = nrm((d,), 1.0)
    w_mod = nrm((DEPTH, d, 6 * d), 0.5 * d ** -0.5)
    b_mod = nrm((DEPTH, 6 * d), 0.02)
    ln_g = 1.0 + nrm((DEPTH, 2, d), 0.02)
    ln_b = nrm((DEPTH, 2, d), 0.02)
    gmlp_w_in = nrm((N_GMLP_LAYERS, d, 2 * GMLP_WIDTH), d ** -0.5)
    gmlp_ln_g = 1.0 + nrm((N_GMLP_LAYERS, GMLP_WIDTH), 0.02)
    gmlp_ln_b = nrm((N_GMLP_LAYERS, GMLP_WIDTH), 0.02)
    gmlp_w_s = nrm((N_GMLP_LAYERS, GMLP_GROUPS, GMLP_CHUNK, GMLP_CHUNK), GMLP_CHUNK ** -0.5)
    gmlp_b_s = 1.0 + nrm((N_GMLP_LAYERS, GMLP_GROUPS, GMLP_CHUNK), 0.02)
    gmlp_w_out = nrm((N_GMLP_LAYERS, GMLP_WIDTH, d), BETA * GMLP_WIDTH ** -0.5)
    ssd_w_in = nrm((N_SSD_LAYERS, d, SSD_IN_DIM), d ** -0.5)
    ssd_conv_w = nrm((N_SSD_LAYERS, SSD_CONV, SSD_CONV_DIM), SSD_CONV ** -0.5)
    ssd_conv_b = nrm((N_SSD_LAYERS, SSD_CONV_DIM), 0.02)
    dt0 = jnp.exp(jax.random.uniform(next(ks), (N_SSD_LAYERS, 2, SSD_HEADS), jnp.float32,
                                     math.log(1e-3), math.log(1e-1)))
    ssd_dt_bias = dt0 + jnp.log(-jnp.expm1(-dt0))
    ssd_a_log = jnp.log(jax.random.uniform(next(ks), (N_SSD_LAYERS, 2, SSD_HEADS), jnp.float32, 1.0, 16.0))
    ssd_d_skip = 1.0 + nrm((N_SSD_LAYERS, SSD_HEADS), 0.1)
    ssd_norm_g = 1.0 + nrm((N_SSD_LAYERS, SSD_INNER), 0.02)
    ssd_w_out = nrm((N_SSD_LAYERS, SSD_INNER, d), BETA * SSD_INNER ** -0.5)
    hgrn_w_in = nrm((N_HGRN_LAYERS, d, HGRN_IN_DIM), d ** -0.5)
    hgrn_lb_logits = nrm((DEPTH, 2, HGRN_KW), 0.5)
    hgrn_norm_g = 1.0 + nrm((N_HGRN_LAYERS, HGRN_VDIM), 0.02)
    hgrn_w_out = nrm((N_HGRN_LAYERS, HGRN_VW, d), BETA * HGRN_VW ** -0.5)
    moe_w_router = nrm((d, N_EXPERTS), d ** -0.5)
    moe_b_router = nrm((N_EXPERTS,), 0.01)
    moe_w_gu = nrm((DEPTH, N_EXPERTS, d, 2 * D_EXPERT), d ** -0.5)
    moe_w_down = nrm((DEPTH, N_EXPERTS, D_EXPERT, d), BETA * D_EXPERT ** -0.5)
    return {'x_prompt': x_prompt, 'x_sample': x_sample, 'state_ssd': state_ssd,
            'state_hgrn': state_hgrn, 'c': c, 'c_ctx': c_ctx, 'w_mod': w_mod, 'b_mod': b_mod,
            'ln_g': ln_g, 'ln_b': ln_b, 'gmlp_w_in': gmlp_w_in, 'gmlp_ln_g': gmlp_ln_g,
            'gmlp_ln_b': gmlp_ln_b, 'gmlp_w_s': gmlp_w_s, 'gmlp_b_s': gmlp_b_s,
            'gmlp_w_out': gmlp_w_out, 'ssd_w_in': ssd_w_in, 'ssd_conv_w': ssd_conv_w,
            'ssd_conv_b': ssd_conv_b, 'ssd_dt_bias': ssd_dt_bias, 'ssd_a_log': ssd_a_log,
            'ssd_d_skip': ssd_d_skip, 'ssd_norm_g': ssd_norm_g, 'ssd_w_out': ssd_w_out,
            'hgrn_w_in': hgrn_w_in, 'hgrn_lb_logits': hgrn_lb_logits, 'hgrn_norm_g': hgrn_norm_g,
            'hgrn_w_out': hgrn_w_out, 'moe_w_router': moe_w_router, 'moe_b_router': moe_b_router,
            'moe_w_gu': moe_w_gu, 'moe_w_down': moe_w_down}


def reference(x_prompt, x_sample, state_ssd, state_hgrn, c, c_ctx, w_mod, b_mod, ln_g, ln_b,
              gmlp_w_in, gmlp_ln_g, gmlp_ln_b, gmlp_w_s, gmlp_b_s, gmlp_w_out,
              ssd_w_in, ssd_conv_w, ssd_conv_b, ssd_dt_bias, ssd_a_log, ssd_d_skip,
              ssd_norm_g, ssd_w_out, hgrn_w_in, hgrn_lb_logits, hgrn_norm_g, hgrn_w_out,
              moe_w_router, moe_b_router, moe_w_gu, moe_w_down):
    weights = (w_mod, b_mod, ln_g, ln_b, gmlp_w_in, gmlp_ln_g, gmlp_ln_b, gmlp_w_s, gmlp_b_s,
               gmlp_w_out, ssd_w_in, ssd_conv_w, ssd_conv_b, ssd_dt_bias, ssd_a_log, ssd_d_skip,
               ssd_norm_g, ssd_w_out, hgrn_w_in, hgrn_lb_logits, hgrn_norm_g, hgrn_w_out,
               moe_w_router, moe_b_router, moe_w_gu, moe_w_down)
    n_ctx = x_prompt.shape[0]
    cond_ctx = jnp.broadcast_to(c_ctx[None, :], (n_ctx, D_MODEL))
    zero_ssd = jnp.zeros((n_ctx, N_SSD_LAYERS, 2, SSD_HEADS, SSD_HEADDIM, SSD_STATE), jnp.float32)
    zero_hgrn = jnp.zeros((n_ctx, N_HGRN_LAYERS, 2, HGRN_HEADS, HGRN_KDIM, HGRN_VDIM), jnp.float32)
    y_prompt, new_state_ssd, new_state_hgrn = run_trunk(x_prompt, cond_ctx, zero_ssd, zero_hgrn, *weights)
    rows = x_sample.shape[1] // GRID_W
    x_lat = x_sample + grid_pos_embed(rows, x_sample.dtype)[None]
    y_sample, _, _ = run_trunk(x_lat, c, state_ssd, state_hgrn, *weights)
    return (y_prompt, y_sample, new_state_ssd, new_state_hgrn)
```

```python
import functools
import math

import jax
import jax.numpy as jnp
from jax import lax
from jax.experimental import pallas as pl
from jax.experimental.pallas import tpu as pltpu

F32 = jnp.float32
BF = jnp.bfloat16

D_MODEL = 1024
DEPTH = 4
GRID_W = 64
N_MIXERS = 3
ALPHA = (2.0 * DEPTH) ** 0.25
LN_EPS = 1e-5

GMLP_CHUNK = 128
GMLP_WIDTH = 2 * D_MODEL
GMLP_GROUPS = 8
GMLP_GROUP_DIM = GMLP_WIDTH // GMLP_GROUPS

SSD_INNER = 2 * D_MODEL
SSD_HEADDIM = 64
SSD_HEADS = SSD_INNER // SSD_HEADDIM
SSD_GROUPS = 4
SSD_STATE = 128
SSD_CHUNK = 128
SSD_CONV_DIM = SSD_INNER + 2 * SSD_GROUPS * SSD_STATE

HGRN_HEADS = 8
HGRN_KDIM = 128
HGRN_VDIM = D_MODEL // HGRN_HEADS
HGRN_KW = HGRN_HEADS * HGRN_KDIM
HGRN_VW = HGRN_HEADS * HGRN_VDIM
HGRN_CHUNK = 64
HGRN_BLOCK = 8

N_EXPERTS = 16
N_EXPERT_GROUPS = 4
EXPERTS_PER_GROUP = N_EXPERTS // N_EXPERT_GROUPS
TOP_K = 2
D_EXPERT = 256

LANES = 128
SUBLANES = 8
VMEM_LIMIT_BYTES = 56 * 1024 * 1024
SQRT_HALF = 0.7071067811865476

SSD_XS_CB = SSD_INNER // LANES
SSD_Z_CB0 = SSD_XS_CB
SSD_BC_CB0 = 2 * SSD_XS_CB
SSD_BC_CB = 2 * SSD_GROUPS * SSD_STATE // LANES
SSD_DT_CB = SSD_BC_CB0 + SSD_BC_CB
SSD_NCB = SSD_DT_CB + 1
SSD_CONV_CB = SSD_XS_CB + SSD_BC_CB

HGRN_NCB = (3 * HGRN_KW + 2 * HGRN_VW) // LANES


def _cparams(n_axes):
    return pltpu.CompilerParams(dimension_semantics=("arbitrary",) * n_axes,
                                vmem_limit_bytes=VMEM_LIMIT_BYTES)


def _resident(shape):
    nd = len(shape)
    return pl.BlockSpec(shape, lambda *_: (0,) * nd, pipeline_mode=pl.Buffered(1))


def _dot(a, b):
    return jnp.dot(a, b, preferred_element_type=F32)


def _dot_nt(a, b):
    return lax.dot_general(a, b, (((1,), (1,)), ((), ())), preferred_element_type=F32)


def _dot_tn(a, b):
    return lax.dot_general(a, b, (((0,), (0,)), ((), ())), preferred_element_type=F32)


def _split3(a):
    a0 = a.astype(BF)
    r1 = a - a0.astype(F32)
    a1 = r1.astype(BF)
    a2 = (r1 - a1.astype(F32)).astype(BF)
    return a0, a1, a2


def _dot_exact_lhs(m_bf, a):
    a0, a1, a2 = _split3(a)
    return _dot(m_bf, a0) + _dot(m_bf, a1) + _dot(m_bf, a2)


def _silu(t):
    return t * jax.nn.sigmoid(t)


def _gelu(t):
    return 0.5 * t * (1.0 + lax.erf(t * SQRT_HALF))


def _softplus(t):
    return jnp.maximum(t, 0.0) + jnp.log1p(jnp.exp(-jnp.abs(t)))


def _layer_norm(v, g, b):
    mu = jnp.mean(v, axis=-1, keepdims=True)
    c = v - mu
    var = jnp.mean(c * c, axis=-1, keepdims=True)
    return c * lax.rsqrt(var + LN_EPS) * g + b


def _mod_index(shared):
    return (lambda b, i: (0, 0, 0)) if shared else (lambda b, i: (b, 0, 0))


MOD_ROWS = 16
MOD_TN = 1536


def _mod_kernel(c_ref, w_ref, b_ref, o_ref):
    c = c_ref[...]
    s = _silu(c).astype(BF)
    o_ref[0] = _dot(s, w_ref[0].astype(BF)) + b_ref[0]


def _modulation(cond, w_mod, b_mod):
    n = 6 * D_MODEL
    return pl.pallas_call(
        _mod_kernel,
        out_shape=jax.ShapeDtypeStruct((DEPTH, MOD_ROWS, n), F32),
        grid=(DEPTH, n // MOD_TN),
        in_specs=[
            pl.BlockSpec((MOD_ROWS, D_MODEL), lambda l, j: (0, 0)),
            pl.BlockSpec((1, D_MODEL, MOD_TN), lambda l, j: (l, 0, j)),
            pl.BlockSpec((1, 1, MOD_TN), lambda l, j: (l, 0, j)),
        ],
        out_specs=pl.BlockSpec((1, MOD_ROWS, MOD_TN), lambda l, j: (l, 0, j)),
        compiler_params=_cparams(2),
        name="modulation",
    )(cond, w_mod, b_mod.reshape(DEPTH, 1, n))


INPROJ_TM = 256
INPROJ_NSPLIT = 8


def _inproj_kernel(x_ref, mod_ref, w_ref, o_ref, *, ncb):
    x = x_ref[0]
    h = (x * (1.0 + mod_ref[0, 1:2, :]) + mod_ref[0, 0:1, :]).astype(BF)
    for c0 in range(0, ncb, INPROJ_NSPLIT):
        n = min(INPROJ_NSPLIT, ncb - c0)
        acc = _dot(h, w_ref[:, c0 * LANES:(c0 + n) * LANES])
        for c in range(n):
            o_ref[0, c0 + c] = acc[:, c * LANES:(c + 1) * LANES]


def _inproj(x, mod, w, shared):
    bsz, L, _ = x.shape
    ncb = w.shape[1] // LANES
    tm = INPROJ_TM
    return pl.pallas_call(
        functools.partial(_inproj_kernel, ncb=ncb),
        out_shape=jax.ShapeDtypeStruct((bsz, ncb, L, LANES), F32),
        grid=(bsz, L // tm),
        in_specs=[
            pl.BlockSpec((1, tm, D_MODEL), lambda b, i: (b, i, 0)),
            pl.BlockSpec((1, 6, D_MODEL), _mod_index(shared)),
            _resident(w.shape),
        ],
        out_specs=pl.BlockSpec((1, ncb, tm, LANES), lambda b, i: (b, 0, i, 0)),
        compiler_params=_cparams(2),
        name="inproj",
    )(x, mod, w)


GMLP_TM = 256
GMLP_NSPLIT = 512


def _gmlp_kernel(*refs, has_pos, tm):
    if has_pos:
        x_ref, pos_ref, *refs = refs
    else:
        x_ref, *refs = refs
    (mod_ref, win_ref, vg_ref, vb_ref, ws_ref, bst_ref, wout_ref, lng_ref, lnb_ref,
     o_ref, u_ref, v_ref, gated_ref) = refs
    x = x_ref[0]
    if has_pos:
        x = x + pos_ref[...]
    h = (x * (1.0 + mod_ref[0, 1:2, :]) + mod_ref[0, 0:1, :]).astype(BF)
    for c0 in range(0, GMLP_WIDTH, GMLP_NSPLIT):
        u_ref[:, c0:c0 + GMLP_NSPLIT] = _gelu(_dot(h, win_ref[:, c0:c0 + GMLP_NSPLIT]))
        v_ref[:, c0:c0 + GMLP_NSPLIT] = _gelu(
            _dot(h, win_ref[:, GMLP_WIDTH + c0:GMLP_WIDTH + c0 + GMLP_NSPLIT]))
    vn = _layer_norm(v_ref[...], vg_ref[...], vb_ref[...]).astype(BF)
    for c in range(tm // GMLP_CHUNK):
        r0 = c * GMLP_CHUNK
        for g in range(GMLP_GROUPS):
            c0 = g * GMLP_GROUP_DIM
            vv = vn[r0:r0 + GMLP_CHUNK, c0:c0 + GMLP_GROUP_DIM]
            mixed = _dot(ws_ref[g], vv) + bst_ref[:, g:g + 1]
            gated_ref[r0:r0 + GMLP_CHUNK, c0:c0 + GMLP_GROUP_DIM] = (
                u_ref[r0:r0 + GMLP_CHUNK, c0:c0 + GMLP_GROUP_DIM] * mixed).astype(BF)
    out = _dot(gated_ref[...], wout_ref[...])
    y = ALPHA * x + mod_ref[0, 2:3, :] * out
    o_ref[0] = _layer_norm(y, lng_ref[...], lnb_ref[...])


def _gmlp_layer(x, pos, mod, shared, w_in, v_g, v_b, w_s, b_st, w_out, ln_g, ln_b):
    bsz, L, _ = x.shape
    tm = GMLP_TM
    has_pos = pos is not None
    in_specs = [pl.BlockSpec((1, tm, D_MODEL), lambda b, i: (b, i, 0))]
    args = [x]
    if has_pos:
        in_specs.append(pl.BlockSpec((tm, D_MODEL), lambda b, i: (i, 0)))
        args.append(pos)
    in_specs += [
        pl.BlockSpec((1, 6, D_MODEL), _mod_index(shared)),
        _resident(w_in.shape), _resident(v_g.shape), _resident(v_b.shape),
        _resident(w_s.shape), _resident(b_st.shape), _resident(w_out.shape),
        _resident(ln_g.shape), _resident(ln_b.shape),
    ]
    args += [mod, w_in, v_g, v_b, w_s, b_st, w_out, ln_g, ln_b]
    return pl.pallas_call(
        functools.partial(_gmlp_kernel, has_pos=has_pos, tm=tm),
        out_shape=jax.ShapeDtypeStruct((bsz, L, D_MODEL), F32),
        grid=(bsz, L // tm),
        in_specs=in_specs,
        out_specs=pl.BlockSpec((1, tm, D_MODEL), lambda b, i: (b, i, 0)),
        scratch_shapes=[pltpu.VMEM((tm, GMLP_WIDTH), F32), pltpu.VMEM((tm, GMLP_WIDTH), F32),
                        pltpu.VMEM((tm, GMLP_WIDTH), BF)],
        compiler_params=_cparams(2),
        name="gmlp_layer",
    )(*args)


def _ssd_scan_kernel(*refs, nc, has_init):
    dirs = []
    for _ in range(2):
        dirs.append(refs[:7])
        refs = refs[7:]
    convw_ref, convb_ref, dtb_ref, alog_ref, dskip_ref, *refs = refs
    if has_init:
        h0_ref, *refs = refs
    yf_ref, yb_ref, st_ref, state_ref, act_ref = refs
    y_refs = (yf_ref, yb_ref)
    q = SSD_CHUNK
    i = pl.program_id(1)

    @pl.when(i == 0)
    def _():
        if has_init:
            state_ref[...] = h0_ref[0, 0]
        else:
            state_ref[...] = jnp.zeros(state_ref.shape, F32)

    row = lax.broadcasted_iota(jnp.int32, (q, q), 0)
    col = lax.broadcasted_iota(jnp.int32, (q, q), 1)
    lo = col < SSD_HEADDIM
    rowlo = row < SSD_HEADDIM
    a_neg = -jnp.exp(alog_ref[...])

    for d in range(2):
        xs_ref, bc_ref, xsp_ref, bcp_ref, xsn_ref, bcn_ref, dt_ref = dirs[d]
        cidx = i if d == 0 else nc - 1 - i
        first = cidx == 0
        last = cidx == nc - 1
        for c in range(SSD_CONV_CB):
            if c < SSD_XS_CB:
                main, pr, nx = xs_ref[0, c], xsp_ref[0, c], xsn_ref[0, c]
            else:
                cc = c - SSD_XS_CB
                main, pr, nx = bc_ref[0, cc], bcp_ref[0, cc], bcn_ref[0, cc]
            pr = jnp.where(first, 0.0, pr[SUBLANES - 1:SUBLANES, :])
            nx = jnp.where(last, 0.0, nx[0:1, :])
            xp = jnp.where(row == 0, pr, pltpu.roll(main, 1, 0))
            xn = jnp.where(row == q - 1, nx, pltpu.roll(main, q - 1, 0))
            w = convw_ref[c]
            act_ref[c] = _silu(w[0:1, :] * xp + w[1:2, :] * main + w[2:3, :] * xn + convb_ref[c])

        dt = _softplus(dt_ref[0, 0] + dtb_ref[...])
        tri = (row >= col) if d == 0 else (row <= col)
        acs = _dot_exact_lhs(jnp.where(tri, 1.0, 0.0).astype(BF), dt * a_neg)
        acs_t = acs.T
        tot = acs[q - 1:q, :] if d == 0 else acs[0:1, :]
        dec_end = jnp.exp(tot - acs) * dt
        e_acs = jnp.exp(acs)
        chunk_decay = jnp.exp(tot)

        for g in range(SSD_GROUPS):
            b_g = act_ref[SSD_XS_CB + g].astype(BF)
            c_g = act_ref[SSD_XS_CB + SSD_GROUPS + g].astype(BF)
            cb = _dot_nt(c_g, b_g)
            for pp in range(SSD_XS_CB // SSD_GROUPS):
                p = g * (SSD_XS_CB // SSD_GROUPS) + pp
                k0 = d * SSD_HEADS + 2 * p
                k1 = k0 + 1
                xblk = act_ref[p]
                ws = []
                for k in (k0, k1):
                    seg = acs[:, k:k + 1] - acs_t[k:k + 1, :]
                    ws.append((cb * jnp.exp(jnp.where(tri, seg, -jnp.inf))).astype(BF))
                xdt = xblk * jnp.where(lo, dt[:, k0:k0 + 1], dt[:, k1:k1 + 1])
                x_bd = jnp.concatenate([jnp.where(lo, xdt, 0.0).astype(BF),
                                        jnp.where(lo, 0.0, xdt).astype(BF)], axis=0)
                y = _dot(jnp.concatenate(ws, axis=1), x_bd)
                s_in = state_ref[d, p]
                y = y + _dot_nt(c_g, s_in.astype(BF)) * jnp.where(
                    lo, e_acs[:, k0:k0 + 1], e_acs[:, k1:k1 + 1])
                if d == 0:
                    y = y + xblk * dskip_ref[p]
                y_refs[d][0, p] = y
                xw = (xblk * jnp.where(lo, dec_end[:, k0:k0 + 1], dec_end[:, k1:k1 + 1])).astype(BF)
                state_ref[d, p] = s_in * jnp.where(
                    rowlo, chunk_decay[:, k0:k0 + 1], chunk_decay[:, k1:k1 + 1]) + _dot_tn(xw, b_g)

    @pl.when(i == nc - 1)
    def _():
        st_ref[0] = state_ref[...]


def _ssd_scan(proj, h0, convw, convb, dtb, alog, dskip):
    bsz, _, L, _ = proj.shape
    q = SSD_CHUNK
    nc = L // q
    halo_max = L // SUBLANES - 1
    hpc = q // SUBLANES

    def specs(cmap):
        prev = lambda b, i: jnp.maximum(cmap(i) * hpc - 1, 0)
        nxt = lambda b, i: jnp.minimum(cmap(i) * hpc + hpc, halo_max)
        bc0 = SSD_BC_CB0 // SSD_BC_CB
        return [
            pl.BlockSpec((1, SSD_XS_CB, q, LANES), lambda b, i: (b, 0, cmap(i), 0)),
            pl.BlockSpec((1, SSD_BC_CB, q, LANES), lambda b, i: (b, bc0, cmap(i), 0)),
            pl.BlockSpec((1, SSD_XS_CB, SUBLANES, LANES), lambda b, i: (b, 0, prev(b, i), 0)),
            pl.BlockSpec((1, SSD_BC_CB, SUBLANES, LANES), lambda b, i: (b, bc0, prev(b, i), 0)),
            pl.BlockSpec((1, SSD_XS_CB, SUBLANES, LANES), lambda b, i: (b, 0, nxt(b, i), 0)),
            pl.BlockSpec((1, SSD_BC_CB, SUBLANES, LANES), lambda b, i: (b, bc0, nxt(b, i), 0)),
            pl.BlockSpec((1, 1, q, LANES), lambda b, i: (b, SSD_DT_CB, cmap(i), 0)),
        ]

    in_specs = specs(lambda i: i) + specs(lambda i: nc - 1 - i)
    args = [proj] * 14
    for a in (convw, convb, dtb, alog, dskip):
        in_specs.append(_resident(a.shape))
        args.append(a)
    has_init = h0 is not None
    st_block = (2, SSD_XS_CB, LANES, SSD_STATE)
    if has_init:
        in_specs.append(pl.BlockSpec((1, 1) + st_block, lambda b, i: (b, 0, 0, 0, 0, 0)))
        args.append(h0)
    y_shape = jax.ShapeDtypeStruct((bsz, SSD_XS_CB, L, LANES), F32)
    return pl.pallas_call(
        functools.partial(_ssd_scan_kernel, nc=nc, has_init=has_init),
        out_shape=(y_shape, y_shape, jax.ShapeDtypeStruct((bsz,) + st_block, F32)),
        grid=(bsz, nc),
        in_specs=in_specs,
        out_specs=(
            pl.BlockSpec((1, SSD_XS_CB, q, LANES), lambda b, i: (b, 0, i, 0)),
            pl.BlockSpec((1, SSD_XS_CB, q, LANES), lambda b, i: (b, 0, nc - 1 - i, 0)),
            pl.BlockSpec((1,) + st_block, lambda b, i: (b, 0, 0, 0, 0)),
        ),
        scratch_shapes=[pltpu.VMEM(st_block, F32), pltpu.VMEM((SSD_CONV_CB, q, LANES), F32)],
        compiler_params=_cparams(2),
        name="ssd_scan",
    )(*args)


OUT_TM = 256


def _ssd_out_kernel(yf_ref, yb_ref, z_ref, x_ref, mod_ref, ng_ref, wout_ref, lng_ref, lnb_ref,
                    o_ref, yz_ref, buf_ref):
    ssq = None
    for c in range(SSD_XS_CB):
        yz = (yf_ref[0, c] + yb_ref[0, c]) * _silu(z_ref[0, c])
        yz_ref[c] = yz
        ssq = yz * yz if ssq is None else ssq + yz * yz
    r = lax.rsqrt(jnp.sum(ssq, axis=-1, keepdims=True) * (1.0 / SSD_INNER) + LN_EPS)
    for c in range(SSD_XS_CB):
        buf_ref[:, c * LANES:(c + 1) * LANES] = (yz_ref[c] * r * ng_ref[c]).astype(BF)
    out = _dot(buf_ref[...], wout_ref[...])
    y = ALPHA * x_ref[0] + mod_ref[0, 2:3, :] * out
    o_ref[0] = _layer_norm(y, lng_ref[...], lnb_ref[...])


def _ssd_out(yf, yb, proj, x, mod, shared, norm_g, w_out, ln_g, ln_b):
    bsz, L, _ = x.shape
    tm = OUT_TM
    blk = pl.BlockSpec((1, SSD_XS_CB, tm, LANES), lambda b, i: (b, 0, i, 0))
    return pl.pallas_call(
        _ssd_out_kernel,
        out_shape=jax.ShapeDtypeStruct((bsz, L, D_MODEL), F32),
        grid=(bsz, L // tm),
        in_specs=[
            blk, blk,
            pl.BlockSpec((1, SSD_XS_CB, tm, LANES), lambda b, i: (b, SSD_Z_CB0 // SSD_XS_CB, i, 0)),
            pl.BlockSpec((1, tm, D_MODEL), lambda b, i: (b, i, 0)),
            pl.BlockSpec((1, 6, D_MODEL), _mod_index(shared)),
            _resident(norm_g.shape), _resident(w_out.shape),
            _resident(ln_g.shape), _resident(ln_b.shape),
        ],
        out_specs=pl.BlockSpec((1, tm, D_MODEL), lambda b, i: (b, i, 0)),
        scratch_shapes=[pltpu.VMEM((SSD_XS_CB, tm, LANES), F32), pltpu.VMEM((tm, SSD_INNER), BF)],
        compiler_params=_cparams(2),
        name="ssd_out",
    )(yf, yb, proj, x, mod, norm_g, w_out, ln_g, ln_b)


def _hgrn_scan_kernel(*refs, nc, layer, has_init):
    dirs = (refs[0:3], refs[3:6])
    lbl_ref, *refs = refs[6:]
    if has_init:
        h0_ref, *refs = refs
    of_ref, ob_ref, st_ref, state_ref = refs
    o_refs = (of_ref, ob_ref)
    q = HGRN_CHUNK
    nb = q // HGRN_BLOCK
    i = pl.program_id(1)

    @pl.when(i == 0)
    def _():
        for d in range(2):
            for h in range(HGRN_HEADS):
                if has_init:
                    state_ref[d, h] = h0_ref[0, 0, d, h].T
                else:
                    state_ref[d, h] = jnp.zeros((HGRN_VDIM, HGRN_KDIM), F32)

    row = lax.broadcasted_iota(jnp.int32, (q, q), 0)
    col = lax.broadcasted_iota(jnp.int32, (q, q), 1)
    sub = lax.broadcasted_iota(jnp.int32, (HGRN_BLOCK, q), 0)
    lane = lax.broadcasted_iota(jnp.int32, (HGRN_BLOCK, q), 1)

    for d in range(2):
        q_ref, f_ref, v_ref = dirs[d]
        tri_bf = jnp.where((row >= col) if d == 0 else (row <= col), 1.0, 0.0).astype(BF)

        def head(h, carry, d=d, q_ref=q_ref, f_ref=f_ref, v_ref=v_ref, tri_bf=tri_bf):
            qh = q_ref[0, h]
            vh = v_ref[0, h]
            logits = [lbl_ref[j, d, h] for j in range(DEPTH)]
            mx = functools.reduce(jnp.maximum, logits)
            es = [jnp.exp(t - mx) for t in logits]
            den = functools.reduce(lambda a, b: a + b, es)
            lb = jnp.zeros_like(den)
            for j in range(1, layer + 1):
                lb = lb + es[j] / den
            f = lb + (1.0 - lb) * jax.nn.sigmoid(f_ref[0, h])
            kk = 1.0 - f
            g = jnp.log(f)
            bcs = _dot_exact_lhs(tri_bf, g)
            tot = bcs[q - 1:q, :] if d == 0 else bcs[0:1, :]
            s_t = state_ref[d, h]
            o = _dot_nt((qh * jnp.exp(bcs)).astype(BF), s_t.astype(BF))
            blocks = []
            for b in range(nb):
                r0 = b * HGRN_BLOCK
                r1 = r0 + HGRN_BLOCK
                bq = bcs[r0:r1]
                qb = qh[r0:r1]
                if d == 0:
                    ref = bcs[r0:r0 + 1] - g[r0:r0 + 1]
                    pre = (0, r0)
                else:
                    ref = bcs[r1 - 1:r1] - g[r1 - 1:r1]
                    pre = (r1, q)
                if pre[1] > pre[0]:
                    kd = kk[pre[0]:pre[1]] * jnp.exp(ref - bcs[pre[0]:pre[1]])
                    parts = []
                    if pre[0] > 0:
                        parts.append(jnp.zeros((pre[0], HGRN_KDIM), F32))
                    parts.append(kd)
                    if pre[1] < q:
                        parts.append(jnp.zeros((q - pre[1], HGRN_KDIM), F32))
                    kd_full = jnp.concatenate(parts, axis=0) if len(parts) > 1 else parts[0]
                    att_b = _dot_nt((qb * jnp.exp(bq - ref)).astype(BF), kd_full.astype(BF))
                else:
                    att_b = jnp.zeros((HGRN_BLOCK, q), F32)
                for j in range(HGRN_BLOCK):
                    s = r0 + j
                    e = jnp.exp(jnp.minimum(bq - bcs[s:s + 1], 0.0))
                    cs = jnp.sum(qb * kk[s:s + 1] * e, axis=1, keepdims=True)
                    valid = (sub >= j) if d == 0 else (sub <= j)
                    att_b = jnp.where(lane == s, jnp.where(valid, cs, 0.0), att_b)
                blocks.append(att_b)
            att = jnp.concatenate(blocks, axis=0)
            o = o + _dot(att.astype(BF), vh.astype(BF))
            o_refs[d][0, h] = o
            kdec = kk * jnp.exp(tot - bcs)
            state_ref[d, h] = s_t * jnp.exp(tot) + _dot_tn(vh.astype(BF), kdec.astype(BF))
            return carry

        lax.fori_loop(0, HGRN_HEADS, head, 0)

    @pl.when(i == nc - 1)
    def _():
        for d in range(2):
            for h in range(HGRN_HEADS):
                st_ref[0, d, h] = state_ref[d, h].T


def _hgrn_scan(proj, h0, lb_logits, layer):
    bsz, _, L, _ = proj.shape
    q = HGRN_CHUNK
    nc = L // q
    nh = HGRN_HEADS

    def specs(cmap, d):
        return [
            pl.BlockSpec((1, nh, q, LANES), lambda b, i: (b, 0, cmap(i), 0)),
            pl.BlockSpec((1, nh, q, LANES), lambda b, i: (b, 1 + d, cmap(i), 0)),
            pl.BlockSpec((1, nh, q, LANES), lambda b, i: (b, 3, cmap(i), 0)),
        ]

    in_specs = specs(lambda i: i, 0) + specs(lambda i: nc - 1 - i, 1) + [_resident(lb_logits.shape)]
    args = [proj] * 6 + [lb_logits]
    has_init = h0 is not None
    st_block = (2, nh, HGRN_KDIM, HGRN_VDIM)
    if has_init:
        in_specs.append(pl.BlockSpec((1, 1) + st_block, lambda b, i: (b, 0, 0, 0, 0, 0)))
        args.append(h0)
    o_shape = jax.ShapeDtypeStruct((bsz, nh, L, LANES), F32)
    return pl.pallas_call(
        functools.partial(_hgrn_scan_kernel, nc=nc, layer=layer, has_init=has_init),
        out_shape=(o_shape, o_shape, jax.ShapeDtypeStruct((bsz,) + st_block, F32)),
        grid=(bsz, nc),
        in_specs=in_specs,
        out_specs=(
            pl.BlockSpec((1, nh, q, LANES), lambda b, i: (b, 0, i, 0)),
            pl.BlockSpec((1, nh, q, LANES), lambda b, i: (b, 0, nc - 1 - i, 0)),
            pl.BlockSpec((1,) + st_block, lambda b, i: (b, 0, 0, 0, 0)),
        ),
        scratch_shapes=[pltpu.VMEM((2, nh, HGRN_VDIM, HGRN_KDIM), F32)],
        compiler_params=_cparams(2),
        name="hgrn_scan",
    )(*args)


def _hgrn_out_kernel(of_ref, ob_ref, g_ref, x_ref, mod_ref, ng_ref, wout_ref, lng_ref, lnb_ref,
                     o_ref, buf_ref):
    for c in range(HGRN_HEADS):
        o = of_ref[0, c] + ob_ref[0, c]
        r = lax.rsqrt(jnp.mean(o * o, axis=-1, keepdims=True) + LN_EPS)
        buf_ref[:, c * LANES:(c + 1) * LANES] = (o * r * ng_ref[...] * _silu(g_ref[0, c])).astype(BF)
    out = _dot(buf_ref[...], wout_ref[...])
    y = ALPHA * x_ref[0] + mod_ref[0, 2:3, :] * out
    o_ref[0] = _layer_norm(y, lng_ref[...], lnb_ref[...])


def _hgrn_out(of, ob, proj, x, mod, shared, norm_g, w_out, ln_g, ln_b):
    bsz, L, _ = x.shape
    tm = OUT_TM
    nh = HGRN_HEADS
    blk = pl.BlockSpec((1, nh, tm, LANES), lambda b, i: (b, 0, i, 0))
    return pl.pallas_call(
        _hgrn_out_kernel,
        out_shape=jax.ShapeDtypeStruct((bsz, L, D_MODEL), F32),
        grid=(bsz, L // tm),
        in_specs=[
            blk, blk,
            pl.BlockSpec((1, nh, tm, LANES), lambda b, i: (b, 4, i, 0)),
            pl.BlockSpec((1, tm, D_MODEL), lambda b, i: (b, i, 0)),
            pl.BlockSpec((1, 6, D_MODEL), _mod_index(shared)),
            _resident(norm_g.shape), _resident(w_out.shape),
            _resident(ln_g.shape), _resident(ln_b.shape),
        ],
        out_specs=pl.BlockSpec((1, tm, D_MODEL), lambda b, i: (b, i, 0)),
        scratch_shapes=[pltpu.VMEM((tm, HGRN_VW), BF)],
        compiler_params=_cparams(2),
        name="hgrn_out",
    )(of, ob, proj, x, mod, norm_g, w_out, ln_g, ln_b)


MOE_TM = 512


def _route(scores, sel):
    s = [scores[e:e + 1] for e in range(N_EXPERTS)]
    v = [sel[e:e + 1] for e in range(N_EXPERTS)]
    top2 = []
    for e in range(N_EXPERTS):
        g0 = (e // EXPERTS_PER_GROUP) * EXPERTS_PER_GROUP
        rank = None
        for j in range(g0, g0 + EXPERTS_PER_GROUP):
            if j == e:
                continue
            beats = jnp.where((v[j] >= v[e]) if j < e else (v[j] > v[e]), 1.0, 0.0)
            rank = beats if rank is None else rank + beats
        top2.append(rank < TOP_K)
    gscore = []
    for g in range(N_EXPERT_GROUPS):
        acc = None
        for e in range(g * EXPERTS_PER_GROUP, (g + 1) * EXPERTS_PER_GROUP):
            t = jnp.where(top2[e], v[e], 0.0)
            acc = t if acc is None else acc + t
        gscore.append(acc)
    best = []
    for g in range(N_EXPERT_GROUPS):
        ok = None
        for g2 in range(N_EXPERT_GROUPS):
            if g2 == g:
                continue
            c = (gscore[g2] < gscore[g]) if g2 < g else (gscore[g2] <= gscore[g])
            ok = c if ok is None else jnp.logical_and(ok, c)
        best.append(ok)
    chosen = [jnp.logical_and(best[e // EXPERTS_PER_GROUP], top2[e]) for e in range(N_EXPERTS)]
    den = None
    for e in range(N_EXPERTS):
        t = jnp.where(chosen[e], s[e], 0.0)
        den = t if den is None else den + t
    return jnp.concatenate([jnp.where(chosen[e], s[e] / den, 0.0) for e in range(N_EXPERTS)], axis=0)


def _moe_kernel(x_ref, mod_ref, wrh_ref, wrl_ref, br_ref, wgu_ref, wd_ref, lng_ref, lnb_ref,
                o_ref, h_ref, cw_ref, acc_ref):
    e = pl.program_id(2)

    @pl.when(e == 0)
    def _():
        h = x_ref[0] * (1.0 + mod_ref[0, 4:5, :]) + mod_ref[0, 3:4, :]
        h_hi = h.astype(BF)
        h_lo = (h - h_hi.astype(F32)).astype(BF)
        h_ref[...] = h_hi
        logits = (_dot_nt(wrh_ref[...], h_hi) + _dot_nt(wrl_ref[...], h_hi)
                  + _dot_nt(wrh_ref[...], h_lo))
        scores = jax.nn.sigmoid(logits)
        cw_ref[...] = _route(scores, scores + br_ref[...]).T
        acc_ref[...] = jnp.zeros(acc_ref.shape, F32)

    gu = _dot(h_ref[...], wgu_ref[0])
    lane = lax.broadcasted_iota(jnp.int32, cw_ref.shape, 1)
    cw_e = jnp.sum(jnp.where(lane == e, cw_ref[...], 0.0), axis=1, keepdims=True)
    hid = (_silu(gu[:, :D_EXPERT]) * gu[:, D_EXPERT:] * cw_e).astype(BF)
    acc_ref[...] += _dot(hid, wd_ref[0])

    @pl.when(e == N_EXPERTS - 1)
    def _():
        y = ALPHA * x_ref[0] + mod_ref[0, 5:6, :] * acc_ref[...]
        o_ref[0] = _layer_norm(y, lng_ref[...], lnb_ref[...])


def _moe_layer(x, mod, shared, wr_hi, wr_lo, b_r, w_gu, w_down, ln_g, ln_b):
    bsz, L, _ = x.shape
    tm = MOE_TM
    mod_idx = (lambda b, i, e: (0, 0, 0)) if shared else (lambda b, i, e: (b, 0, 0))
    const2 = lambda b, i, e: (0, 0)
    return pl.pallas_call(
        _moe_kernel,
        out_shape=jax.ShapeDtypeStruct((bsz, L, D_MODEL), F32),
        grid=(bsz, L // tm, N_EXPERTS),
        in_specs=[
            pl.BlockSpec((1, tm, D_MODEL), lambda b, i, e: (b, i, 0)),
            pl.BlockSpec((1, 6, D_MODEL), mod_idx),
            pl.BlockSpec(wr_hi.shape, const2), pl.BlockSpec(wr_lo.shape, const2),
            pl.BlockSpec(b_r.shape, const2),
            pl.BlockSpec((1, D_MODEL, 2 * D_EXPERT), lambda b, i, e: (e, 0, 0)),
            pl.BlockSpec((1, D_EXPERT, D_MODEL), lambda b, i, e: (e, 0, 0)),
            pl.BlockSpec(ln_g.shape, const2), pl.BlockSpec(ln_b.shape, const2),
        ],
        out_specs=pl.BlockSpec((1, tm, D_MODEL), lambda b, i, e: (b, i, 0)),
        scratch_shapes=[pltpu.VMEM((tm, D_MODEL), BF), pltpu.VMEM((tm, N_EXPERTS), F32),
                        pltpu.VMEM((tm, D_MODEL), F32)],
        compiler_params=_cparams(3),
        name="moe_layer",
    )(x, mod, wr_hi, wr_lo, b_r, w_gu, w_down, ln_g, ln_b)


def _grid_pos_embed(rows):
    quarter = D_MODEL // 4
    freq = jnp.exp(-math.log(10000.0) / quarter * jnp.arange(quarter, dtype=F32))
    r = jnp.repeat(jnp.arange(rows, dtype=F32), GRID_W)
    col = jnp.tile(jnp.arange(GRID_W, dtype=F32), rows)
    ar = r[:, None] * freq[None, :]
    ac = col[:, None] * freq[None, :]
    return jnp.concatenate([jnp.sin(ar), jnp.cos(ar), jnp.sin(ac), jnp.cos(ac)], axis=-1)


def _row(a):
    return a.reshape(1, -1)


def _col_blocks(a):
    n = a.shape[-1] // LANES
    return jnp.moveaxis(a.reshape(a.shape[:-1] + (n, LANES)), -2, 0)


def _prep_weights(p):
    w = {}
    w["gmlp_w_in"] = p["gmlp_w_in"].astype(BF)
    w["gmlp_w_s"] = p["gmlp_w_s"].astype(BF)
    w["gmlp_b_st"] = jnp.swapaxes(p["gmlp_b_s"], 1, 2)
    w["gmlp_w_out"] = p["gmlp_w_out"].astype(BF)
    wi = p["ssd_w_in"]
    n_l = wi.shape[0]
    pad = jnp.zeros((n_l, D_MODEL, LANES - 2 * SSD_HEADS), F32)
    w["ssd_w_in"] = jnp.concatenate([
        wi[:, :, SSD_INNER:2 * SSD_INNER],
        wi[:, :, :SSD_INNER],
        wi[:, :, 2 * SSD_INNER:SSD_INNER + SSD_CONV_DIM],
        wi[:, :, SSD_INNER + SSD_CONV_DIM:], pad], axis=-1).astype(BF)
    w["ssd_conv_w"] = jnp.stack([_col_blocks(t) for t in p["ssd_conv_w"]])
    w["ssd_conv_b"] = jnp.stack([_col_blocks(_row(t)) for t in p["ssd_conv_b"]])
    zpad = jnp.zeros((n_l, LANES - 2 * SSD_HEADS), F32)
    w["ssd_dt_bias"] = jnp.concatenate([p["ssd_dt_bias"].reshape(n_l, -1), zpad], axis=-1)[:, None, :]
    w["ssd_a_log"] = jnp.concatenate([p["ssd_a_log"].reshape(n_l, -1), zpad], axis=-1)[:, None, :]
    w["ssd_d_skip"] = jnp.stack([_col_blocks(_row(jnp.repeat(t, SSD_HEADDIM))) for t in p["ssd_d_skip"]])
    w["ssd_norm_g"] = jnp.stack([_col_blocks(_row(t)) for t in p["ssd_norm_g"]])
    w["ssd_w_out"] = p["ssd_w_out"].astype(BF)
    w["hgrn_w_in"] = p["hgrn_w_in"].astype(BF)
    w["hgrn_lb_logits"] = p["hgrn_lb_logits"].reshape(DEPTH, 2, HGRN_HEADS, 1, HGRN_KDIM)
    w["hgrn_w_out"] = p["hgrn_w_out"].astype(BF)
    wr_t = p["moe_w_router"].T
    wr_hi = wr_t.astype(BF)
    w["moe_wr_hi"] = wr_hi
    w["moe_wr_lo"] = (wr_t - wr_hi.astype(F32)).astype(BF)
    w["moe_b_r"] = p["moe_b_router"].reshape(N_EXPERTS, 1)
    w["moe_w_gu"] = p["moe_w_gu"].astype(BF)
    w["moe_w_down"] = p["moe_w_down"].astype(BF)
    return w


def _run_trunk(x, pos, mods, shared, ssd_init, hgrn_init, p, w):
    bsz, L, _ = x.shape
    flat = (1, bsz * L, D_MODEL) if shared else x.shape
    ssd_states, hgrn_states = [], []
    ia = ib = ic = 0
    for li in range(DEPTH):
        mod = mods[li]
        ln_g1, ln_b1 = _row(p["ln_g"][li, 0]), _row(p["ln_b"][li, 0])
        ln_g2, ln_b2 = _row(p["ln_g"][li, 1]), _row(p["ln_b"][li, 1])
        kind = li % N_MIXERS
        if kind == 0:
            x = _gmlp_layer(x.reshape(flat), pos if li == 0 else None, mod, shared,
                            w["gmlp_w_in"][ia], _row(p["gmlp_ln_g"][ia]), _row(p["gmlp_ln_b"][ia]),
                            w["gmlp_w_s"][ia], w["gmlp_b_st"][ia], w["gmlp_w_out"][ia],
                            ln_g1, ln_b1).reshape(bsz, L, D_MODEL)
            ia += 1
        elif kind == 1:
            proj = _inproj(x, mod, w["ssd_w_in"][ib], shared)
            h0 = None if ssd_init is None else ssd_init.reshape(
                bsz, -1, 2, SSD_XS_CB, LANES, SSD_STATE)[:, ib:ib + 1]
            yf, yb, st = _ssd_scan(proj, h0, w["ssd_conv_w"][ib], w["ssd_conv_b"][ib],
                                   w["ssd_dt_bias"][ib], w["ssd_a_log"][ib], w["ssd_d_skip"][ib])
            ssd_states.append(st.reshape(bsz, 2, SSD_HEADS, SSD_HEADDIM, SSD_STATE))
            x = _ssd_out(yf, yb, proj, x, mod, shared, w["ssd_norm_g"][ib], w["ssd_w_out"][ib],
                         ln_g1, ln_b1)
            ib += 1
        else:
            proj = _inproj(x, mod, w["hgrn_w_in"][ic], shared)
            h0 = None if hgrn_init is None else hgrn_init[:, ic:ic + 1]
            of, ob, st = _hgrn_scan(proj, h0, w["hgrn_lb_logits"], li)
            hgrn_states.append(st)
            x = _hgrn_out(of, ob, proj, x, mod, shared, _row(p["hgrn_norm_g"][ic]),
                          w["hgrn_w_out"][ic], ln_g1, ln_b1)
            ic += 1
        x = _moe_layer(x.reshape(flat), mod, shared, w["moe_wr_hi"], w["moe_wr_lo"], w["moe_b_r"],
                       w["moe_w_gu"][li], w["moe_w_down"][li], ln_g2, ln_b2).reshape(bsz, L, D_MODEL)
    return x, jnp.stack(ssd_states, axis=1), jnp.stack(hgrn_states, axis=1)


def kernel(x_prompt, x_sample, state_ssd, state_hgrn, c, c_ctx, w_mod, b_mod, ln_g, ln_b, gmlp_w_in, gmlp_ln_g, gmlp_ln_b, gmlp_w_s, gmlp_b_s, gmlp_w_out, ssd_w_in, ssd_conv_w, ssd_conv_b, ssd_dt_bias, ssd_a_log, ssd_d_skip, ssd_norm_g, ssd_w_out, hgrn_w_in, hgrn_lb_logits, hgrn_norm_g, hgrn_w_out, moe_w_router, moe_b_router, moe_w_gu, moe_w_down):
    p = dict(ln_g=ln_g, ln_b=ln_b, gmlp_w_in=gmlp_w_in, gmlp_ln_g=gmlp_ln_g, gmlp_ln_b=gmlp_ln_b,
             gmlp_w_s=gmlp_w_s, gmlp_b_s=gmlp_b_s, gmlp_w_out=gmlp_w_out, ssd_w_in=ssd_w_in,
             ssd_conv_w=ssd_conv_w, ssd_conv_b=ssd_conv_b, ssd_dt_bias=ssd_dt_bias,
             ssd_a_log=ssd_a_log, ssd_d_skip=ssd_d_skip, ssd_norm_g=ssd_norm_g, ssd_w_out=ssd_w_out,
             hgrn_w_in=hgrn_w_in, hgrn_lb_logits=hgrn_lb_logits, hgrn_norm_g=hgrn_norm_g,
             hgrn_w_out=hgrn_w_out, moe_w_router=moe_w_router, moe_b_router=moe_b_router,
             moe_w_gu=moe_w_gu, moe_w_down=moe_w_down)
    w = _prep_weights(p)
    n_lat = c.shape[0]
    cond = jnp.concatenate([c, c_ctx[None, :],
                            jnp.zeros((MOD_ROWS - n_lat - 1, D_MODEL), F32)], axis=0)
    mods = _modulation(cond, w_mod, b_mod).reshape(DEPTH, MOD_ROWS, 6, D_MODEL)
    mods_lat = [mods[li, :n_lat] for li in range(DEPTH)]
    mods_ctx = [mods[li, n_lat:n_lat + 1] for li in range(DEPTH)]
    y_prompt, new_ssd, new_hgrn = _run_trunk(x_prompt, None, mods_ctx, True, None, None, p, w)
    pos = _grid_pos_embed(x_sample.shape[1] // GRID_W)
    y_sample, _, _ = _run_trunk(x_sample, pos, mods_lat, False, state_ssd, state_hgrn, p, w)
    return (y_prompt, y_sample, new_ssd, new_hgrn)
```

```python
import functools
import math

import jax
import jax.numpy as jnp
from jax import lax
from jax.experimental import pallas as pl
from jax.experimental.pallas import tpu as pltpu

F32 = jnp.float32
BF = jnp.bfloat16

D_MODEL = 1024
DEPTH = 4
GRID_W = 64
N_MIXERS = 3
ALPHA = (2.0 * DEPTH) ** 0.25
LN_EPS = 1e-5

GMLP_CHUNK = 128
GMLP_WIDTH = 2 * D_MODEL
GMLP_GROUPS = 8
GMLP_GROUP_DIM = GMLP_WIDTH // GMLP_GROUPS

SSD_INNER = 2 * D_MODEL
SSD_HEADDIM = 64
SSD_HEADS = SSD_INNER // SSD_HEADDIM
SSD_GROUPS = 4
SSD_STATE = 128
SSD_CHUNK = 128
SSD_CONV_DIM = SSD_INNER + 2 * SSD_GROUPS * SSD_STATE

HGRN_HEADS = 8
HGRN_KDIM = 128
HGRN_VDIM = D_MODEL // HGRN_HEADS
HGRN_KW = HGRN_HEADS * HGRN_KDIM
HGRN_VW = HGRN_HEADS * HGRN_VDIM
HGRN_CHUNK = 64
HGRN_BLOCK = 8

N_EXPERTS = 16
N_EXPERT_GROUPS = 4
EXPERTS_PER_GROUP = N_EXPERTS // N_EXPERT_GROUPS
TOP_K = 2
D_EXPERT = 256

LANES = 128
SUBLANES = 8
VMEM_LIMIT_BYTES = 56 * 1024 * 1024
SQRT_HALF = 0.7071067811865476
LOG2_E = 1.4426950408889634

SSD_XS_CB = SSD_INNER // LANES
SSD_Z_CB0 = SSD_XS_CB
SSD_BC_CB0 = 2 * SSD_XS_CB
SSD_BC_CB = 2 * SSD_GROUPS * SSD_STATE // LANES
SSD_DT_CB = SSD_BC_CB0 + SSD_BC_CB
SSD_NCB = SSD_DT_CB + 1
SSD_CONV_CB = SSD_XS_CB + SSD_BC_CB
SSD_WIDE_SPLIT = 512

HGRN_NCB = (3 * HGRN_KW + 2 * HGRN_VW) // LANES


def _cparams(n_axes):
    return pltpu.CompilerParams(dimension_semantics=("arbitrary",) * n_axes,
                                vmem_limit_bytes=VMEM_LIMIT_BYTES)


def _resident(shape):
    nd = len(shape)
    return pl.BlockSpec(shape, lambda *_: (0,) * nd, pipeline_mode=pl.Buffered(1))


def _dot(a, b):
    return jnp.dot(a, b, preferred_element_type=F32)


def _dot_nt(a, b):
    return lax.dot_general(a, b, (((1,), (1,)), ((), ())), preferred_element_type=F32)


def _dot_tn(a, b):
    return lax.dot_general(a, b, (((0,), (0,)), ((), ())), preferred_element_type=F32)


def _split3(a):
    a0 = a.astype(BF)
    r1 = a - a0.astype(F32)
    a1 = r1.astype(BF)
    a2 = (r1 - a1.astype(F32)).astype(BF)
    return a0, a1, a2


def _dot_exact_lhs(m_bf, a):
    a0, a1, a2 = _split3(a)
    return _dot(m_bf, a0) + _dot(m_bf, a1) + _dot(m_bf, a2)


def _silu(t):
    return t * jax.nn.sigmoid(t)


def _gelu(t):
    return 0.5 * t * (1.0 + lax.erf(t * SQRT_HALF))


def _softplus(t):
    return jnp.maximum(t, 0.0) + jnp.log1p(jnp.exp(-jnp.abs(t)))


def _layer_norm(v, g, b):
    mu = jnp.mean(v, axis=-1, keepdims=True)
    c = v - mu
    var = jnp.mean(c * c, axis=-1, keepdims=True)
    return c * lax.rsqrt(var + LN_EPS) * g + b


def _mod_index(shared):
    return (lambda b, i: (0, 0, 0)) if shared else (lambda b, i: (b, 0, 0))


MOD_ROWS = 16
MOD_TN = 1536


def _mod_kernel(c_ref, w_ref, b_ref, o_ref):
    c = c_ref[...]
    s = _silu(c).astype(BF)
    o_ref[0] = _dot(s, w_ref[0].astype(BF)) + b_ref[0]


def _modulation(cond, w_mod, b_mod):
    n = 6 * D_MODEL
    return pl.pallas_call(
        _mod_kernel,
        out_shape=jax.ShapeDtypeStruct((DEPTH, MOD_ROWS, n), F32),
        grid=(DEPTH, n // MOD_TN),
        in_specs=[
            pl.BlockSpec((MOD_ROWS, D_MODEL), lambda l, j: (0, 0)),
            pl.BlockSpec((1, D_MODEL, MOD_TN), lambda l, j: (l, 0, j)),
            pl.BlockSpec((1, 1, MOD_TN), lambda l, j: (l, 0, j)),
        ],
        out_specs=pl.BlockSpec((1, MOD_ROWS, MOD_TN), lambda l, j: (l, 0, j)),
        compiler_params=_cparams(2),
        name="modulation",
    )(cond, w_mod, b_mod.reshape(DEPTH, 1, n))


INPROJ_TM = 256
INPROJ_NSPLIT = 8


def _inproj_kernel(x_ref, mod_ref, w_ref, o_ref, *, ncb):
    x = x_ref[0]
    h = (x * (1.0 + mod_ref[0, 1:2, :]) + mod_ref[0, 0:1, :]).astype(BF)
    for c0 in range(0, ncb, INPROJ_NSPLIT):
        n = min(INPROJ_NSPLIT, ncb - c0)
        acc = _dot(h, w_ref[:, c0 * LANES:(c0 + n) * LANES])
        for c in range(n):
            o_ref[0, c0 + c] = acc[:, c * LANES:(c + 1) * LANES]


def _inproj(x, mod, w, shared):
    bsz, L, _ = x.shape
    ncb = w.shape[1] // LANES
    tm = INPROJ_TM
    return pl.pallas_call(
        functools.partial(_inproj_kernel, ncb=ncb),
        out_shape=jax.ShapeDtypeStruct((bsz, ncb, L, LANES), F32),
        grid=(bsz, L // tm),
        in_specs=[
            pl.BlockSpec((1, tm, D_MODEL), lambda b, i: (b, i, 0)),
            pl.BlockSpec((1, 6, D_MODEL), _mod_index(shared)),
            _resident(w.shape),
        ],
        out_specs=pl.BlockSpec((1, ncb, tm, LANES), lambda b, i: (b, 0, i, 0)),
        compiler_params=_cparams(2),
        name="inproj",
    )(x, mod, w)


GMLP_TM = 256
GMLP_NSPLIT = 512


def _gmlp_kernel(*refs, has_pos, tm):
    if has_pos:
        x_ref, pos_ref, *refs = refs
    else:
        x_ref, *refs = refs
    (mod_ref, win_ref, vg_ref, vb_ref, ws_ref, bst_ref, wout_ref, lng_ref, lnb_ref,
     o_ref, u_ref, v_ref, gated_ref) = refs
    x = x_ref[0]
    if has_pos:
        x = x + pos_ref[...]
    h = (x * (1.0 + mod_ref[0, 1:2, :]) + mod_ref[0, 0:1, :]).astype(BF)
    for c0 in range(0, GMLP_WIDTH, GMLP_NSPLIT):
        u_ref[:, c0:c0 + GMLP_NSPLIT] = _gelu(_dot(h, win_ref[:, c0:c0 + GMLP_NSPLIT]))
        v_ref[:, c0:c0 + GMLP_NSPLIT] = _gelu(
            _dot(h, win_ref[:, GMLP_WIDTH + c0:GMLP_WIDTH + c0 + GMLP_NSPLIT]))
    vn = _layer_norm(v_ref[...], vg_ref[...], vb_ref[...]).astype(BF)
    for c in range(tm // GMLP_CHUNK):
        r0 = c * GMLP_CHUNK
        for g in range(GMLP_GROUPS):
            c0 = g * GMLP_GROUP_DIM
            vv = vn[r0:r0 + GMLP_CHUNK, c0:c0 + GMLP_GROUP_DIM]
            mixed = _dot(ws_ref[g], vv) + bst_ref[:, g:g + 1]
            gated_ref[r0:r0 + GMLP_CHUNK, c0:c0 + GMLP_GROUP_DIM] = (
                u_ref[r0:r0 + GMLP_CHUNK, c0:c0 + GMLP_GROUP_DIM] * mixed).astype(BF)
    out = _dot(gated_ref[...], wout_ref[...])
    y = ALPHA * x + mod_ref[0, 2:3, :] * out
    o_ref[0] = _layer_norm(y, lng_ref[...], lnb_ref[...])


def _gmlp_layer(x, pos, mod, shared, w_in, v_g, v_b, w_s, b_st, w_out, ln_g, ln_b):
    bsz, L, _ = x.shape
    tm = GMLP_TM
    has_pos = pos is not None
    in_specs = [pl.BlockSpec((1, tm, D_MODEL), lambda b, i: (b, i, 0))]
    args = [x]
    if has_pos:
        in_specs.append(pl.BlockSpec((tm, D_MODEL), lambda b, i: (i, 0)))
        args.append(pos)
    in_specs += [
        pl.BlockSpec((1, 6, D_MODEL), _mod_index(shared)),
        _resident(w_in.shape), _resident(v_g.shape), _resident(v_b.shape),
        _resident(w_s.shape), _resident(b_st.shape), _resident(w_out.shape),
        _resident(ln_g.shape), _resident(ln_b.shape),
    ]
    args += [mod, w_in, v_g, v_b, w_s, b_st, w_out, ln_g, ln_b]
    return pl.pallas_call(
        functools.partial(_gmlp_kernel, has_pos=has_pos, tm=tm),
        out_shape=jax.ShapeDtypeStruct((bsz, L, D_MODEL), F32),
        grid=(bsz, L // tm),
        in_specs=in_specs,
        out_specs=pl.BlockSpec((1, tm, D_MODEL), lambda b, i: (b, i, 0)),
        scratch_shapes=[pltpu.VMEM((tm, GMLP_WIDTH), F32), pltpu.VMEM((tm, GMLP_WIDTH), F32),
                        pltpu.VMEM((tm, GMLP_WIDTH), BF)],
        compiler_params=_cparams(2),
        name="gmlp_layer",
    )(*args)


def _ssd_scan_kernel(*refs, nc, has_init):
    dirs = []
    for _ in range(2):
        dirs.append(refs[:7])
        refs = refs[7:]
    convw_ref, convb_ref, dtb_ref, alog_ref, dskip_ref, expand_ref, *refs = refs
    if has_init:
        h0_ref, *refs = refs
    yf_ref, yb_ref, st_ref, state_ref, act_ref, acs_ref, acst_ref, wide_ref = refs
    y_refs = (yf_ref, yb_ref)
    q = SSD_CHUNK
    i = pl.program_id(1)

    @pl.when(i == 0)
    def _():
        if has_init:
            state_ref[...] = h0_ref[0, 0]
        else:
            state_ref[...] = jnp.zeros(state_ref.shape, F32)

    row = lax.broadcasted_iota(jnp.int32, (q, q), 0)
    col = lax.broadcasted_iota(jnp.int32, (q, q), 1)
    lo = col < SSD_HEADDIM
    rowlo = row < SSD_HEADDIM
    a_neg = -jnp.exp(alog_ref[...]) * LOG2_E

    for d in range(2):
        xs_ref, bc_ref, xsp_ref, bcp_ref, xsn_ref, bcn_ref, dt_ref = dirs[d]
        cidx = i if d == 0 else nc - 1 - i
        first = cidx == 0
        last = cidx == nc - 1
        for c in range(SSD_CONV_CB):
            if c < SSD_XS_CB:
                main, pr, nx = xs_ref[0, c], xsp_ref[0, c], xsn_ref[0, c]
            else:
                cc = c - SSD_XS_CB
                main, pr, nx = bc_ref[0, cc], bcp_ref[0, cc], bcn_ref[0, cc]
            pr = jnp.where(first, 0.0, pr[SUBLANES - 1:SUBLANES, :])
            nx = jnp.where(last, 0.0, nx[0:1, :])
            xp = jnp.where(row == 0, pr, pltpu.roll(main, 1, 0))
            xn = jnp.where(row == q - 1, nx, pltpu.roll(main, q - 1, 0))
            w = convw_ref[c]
            act_ref[c] = _silu(w[0:1, :] * xp + w[1:2, :] * main + w[2:3, :] * xn + convb_ref[c])

        dt = _softplus(dt_ref[0, 0] + dtb_ref[...])
        tri = (row >= col) if d == 0 else (row <= col)
        acs = _dot_exact_lhs(jnp.where(tri, 1.0, 0.0).astype(BF), dt * a_neg)
        acs_ref[d] = acs
        acst_ref[d] = acs.T
        tot = acs[q - 1:q, :] if d == 0 else acs[0:1, :]
        chunk_decay = jnp.exp2(tot)
        cols = jnp.concatenate([dt, jnp.exp2(acs), jnp.exp2(tot - acs) * dt], axis=0)
        c_hi = cols.astype(BF)
        c_lo = (cols - c_hi.astype(F32)).astype(BF)
        for c0 in range(0, SSD_INNER, SSD_WIDE_SPLIT):
            e_blk = expand_ref[d, :, c0:c0 + SSD_WIDE_SPLIT]
            wide_ref[d, :, c0:c0 + SSD_WIDE_SPLIT] = _dot(c_hi, e_blk) + _dot(c_lo, e_blk)

        for g in range(SSD_GROUPS):
            b_g = act_ref[SSD_XS_CB + g].astype(BF)
            c_g = act_ref[SSD_XS_CB + SSD_GROUPS + g].astype(BF)
            cb = _dot_nt(c_g, b_g)
            for pp in range(SSD_XS_CB // SSD_GROUPS):
                p = g * (SSD_XS_CB // SSD_GROUPS) + pp
                k0 = d * SSD_HEADS + 2 * p
                k1 = k0 + 1
                xblk = act_ref[p]
                ps = slice(p * LANES, (p + 1) * LANES)
                ws = []
                for k in (k0, k1):
                    seg = acs_ref[d, :, k:k + 1] - acst_ref[d, k:k + 1, :]
                    ws.append((cb * jnp.exp2(jnp.where(tri, seg, -jnp.inf))).astype(BF))
                xdt = xblk * wide_ref[d, 0:q, ps]
                x_bd = jnp.concatenate([jnp.where(lo, xdt, 0.0).astype(BF),
                                        jnp.where(lo, 0.0, xdt).astype(BF)], axis=0)
                y = _dot(jnp.concatenate(ws, axis=1), x_bd)
                s_in = state_ref[d, p]
                y = y + _dot_nt(c_g, s_in.astype(BF)) * wide_ref[d, q:2 * q, ps]
                if d == 0:
                    y = y + xblk * dskip_ref[p]
                y_refs[d][0, p] = y
                xw = (xblk * wide_ref[d, 2 * q:3 * q, ps]).astype(BF)
                state_ref[d, p] = s_in * jnp.where(
                    rowlo, chunk_decay[:, k0:k0 + 1], chunk_decay[:, k1:k1 + 1]) + _dot_tn(xw, b_g)

    @pl.when(i == nc - 1)
    def _():
        st_ref[0] = state_ref[...]


def _ssd_expand_matrix():
    k = jnp.arange(LANES)[None, :, None]
    head = (jnp.arange(SSD_INNER) // SSD_HEADDIM)[None, None, :]
    d = jnp.arange(2)[:, None, None]
    return (k == d * SSD_HEADS + head).astype(BF)


def _ssd_scan(proj, h0, convw, convb, dtb, alog, dskip):
    bsz, _, L, _ = proj.shape
    q = SSD_CHUNK
    nc = L // q
    halo_max = L // SUBLANES - 1
    hpc = q // SUBLANES

    def specs(cmap):
        prev = lambda b, i: jnp.maximum(cmap(i) * hpc - 1, 0)
        nxt = lambda b, i: jnp.minimum(cmap(i) * hpc + hpc, halo_max)
        bc0 = SSD_BC_CB0 // SSD_BC_CB
        return [
            pl.BlockSpec((1, SSD_XS_CB, q, LANES), lambda b, i: (b, 0, cmap(i), 0)),
            pl.BlockSpec((1, SSD_BC_CB, q, LANES), lambda b, i: (b, bc0, cmap(i), 0)),
            pl.BlockSpec((1, SSD_XS_CB, SUBLANES, LANES), lambda b, i: (b, 0, prev(b, i), 0)),
            pl.BlockSpec((1, SSD_BC_CB, SUBLANES, LANES), lambda b, i: (b, bc0, prev(b, i), 0)),
            pl.BlockSpec((1, SSD_XS_CB, SUBLANES, LANES), lambda b, i: (b, 0, nxt(b, i), 0)),
            pl.BlockSpec((1, SSD_BC_CB, SUBLANES, LANES), lambda b, i: (b, bc0, nxt(b, i), 0)),
            pl.BlockSpec((1, 1, q, LANES), lambda b, i: (b, SSD_DT_CB, cmap(i), 0)),
        ]

    in_specs = specs(lambda i: i) + specs(lambda i: nc - 1 - i)
    args = [proj] * 14
    for a in (convw, convb, dtb, alog, dskip, _ssd_expand_matrix()):
        in_specs.append(_resident(a.shape))
        args.append(a)
    has_init = h0 is not None
    st_block = (2, SSD_XS_CB, LANES, SSD_STATE)
    if has_init:
        in_specs.append(pl.BlockSpec((1, 1) + st_block, lambda b, i: (b, 0, 0, 0, 0, 0)))
        args.append(h0)
    y_shape = jax.ShapeDtypeStruct((bsz, SSD_XS_CB, L, LANES), F32)
    return pl.pallas_call(
        functools.partial(_ssd_scan_kernel, nc=nc, has_init=has_init),
        out_shape=(y_shape, y_shape, jax.ShapeDtypeStruct((bsz,) + st_block, F32)),
        grid=(bsz, nc),
        in_specs=in_specs,
        out_specs=(
            pl.BlockSpec((1, SSD_XS_CB, q, LANES), lambda b, i: (b, 0, i, 0)),
            pl.BlockSpec((1, SSD_XS_CB, q, LANES), lambda b, i: (b, 0, nc - 1 - i, 0)),
            pl.BlockSpec((1,) + st_block, lambda b, i: (b, 0, 0, 0, 0)),
        ),
        scratch_shapes=[pltpu.VMEM(st_block, F32), pltpu.VMEM((SSD_CONV_CB, q, LANES), F32),
                        pltpu.VMEM((2, q, LANES), F32), pltpu.VMEM((2, q, LANES), F32),
                        pltpu.VMEM((2, 3 * q, SSD_INNER), F32)],
        compiler_params=_cparams(2),
        name="ssd_scan",
    )(*args)


OUT_TM = 256


def _ssd_out_kernel(yf_ref, yb_ref, z_ref, x_ref, mod_ref, ng_ref, wout_ref, lng_ref, lnb_ref,
                    o_ref, yz_ref, buf_ref):
    ssq = None
    for c in range(SSD_XS_CB):
        yz = (yf_ref[0, c] + yb_ref[0, c]) * _silu(z_ref[0, c])
        yz_ref[c] = yz
        ssq = yz * yz if ssq is None else ssq + yz * yz
    r = lax.rsqrt(jnp.sum(ssq, axis=-1, keepdims=True) * (1.0 / SSD_INNER) + LN_EPS)
    for c in range(SSD_XS_CB):
        buf_ref[:, c * LANES:(c + 1) * LANES] = (yz_ref[c] * r * ng_ref[c]).astype(BF)
    out = _dot(buf_ref[...], wout_ref[...])
    y = ALPHA * x_ref[0] + mod_ref[0, 2:3, :] * out
    o_ref[0] = _layer_norm(y, lng_ref[...], lnb_ref[...])


def _ssd_out(yf, yb, proj, x, mod, shared, norm_g, w_out, ln_g, ln_b):
    bsz, L, _ = x.shape
    tm = OUT_TM
    blk = pl.BlockSpec((1, SSD_XS_CB, tm, LANES), lambda b, i: (b, 0, i, 0))
    return pl.pallas_call(
        _ssd_out_kernel,
        out_shape=jax.ShapeDtypeStruct((bsz, L, D_MODEL), F32),
        grid=(bsz, L // tm),
        in_specs=[
            blk, blk,
            pl.BlockSpec((1, SSD_XS_CB, tm, LANES), lambda b, i: (b, SSD_Z_CB0 // SSD_XS_CB, i, 0)),
            pl.BlockSpec((1, tm, D_MODEL), lambda b, i: (b, i, 0)),
            pl.BlockSpec((1, 6, D_MODEL), _mod_index(shared)),
            _resident(norm_g.shape), _resident(w_out.shape),
            _resident(ln_g.shape), _resident(ln_b.shape),
        ],
        out_specs=pl.BlockSpec((1, tm, D_MODEL), lambda b, i: (b, i, 0)),
        scratch_shapes=[pltpu.VMEM((SSD_XS_CB, tm, LANES), F32), pltpu.VMEM((tm, SSD_INNER), BF)],
        compiler_params=_cparams(2),
        name="ssd_out",
    )(yf, yb, proj, x, mod, norm_g, w_out, ln_g, ln_b)


def _hgrn_scan_kernel(*refs, nc, layer, has_init):
    dirs = (refs[0:3], refs[3:6])
    lbl_ref, *refs = refs[6:]
    if has_init:
        h0_ref, *refs = refs
    of_ref, ob_ref, st_ref, state_ref, kk_ref, bcs_ref = refs
    o_refs = (of_ref, ob_ref)
    q = HGRN_CHUNK
    blk = HGRN_BLOCK
    nb = q // blk
    i = pl.program_id(1)

    @pl.when(i == 0)
    def _():
        for d in range(2):
            for h in range(HGRN_HEADS):
                if has_init:
                    state_ref[d, h] = h0_ref[0, 0, d, h].T
                else:
                    state_ref[d, h] = jnp.zeros((HGRN_VDIM, HGRN_KDIM), F32)

    row = lax.broadcasted_iota(jnp.int32, (q, q), 0)
    col = lax.broadcasted_iota(jnp.int32, (q, q), 1)
    lane = lax.broadcasted_iota(jnp.int32, (blk, q), 1)
    lane_blk = lane // blk
    ones_bf = jnp.ones((HGRN_KDIM, q), BF)

    for d in range(2):
        q_ref, f_ref, v_ref = dirs[d]
        tri = (row >= col) if d == 0 else (row <= col)
        logits = [lbl_ref[j, d] for j in range(DEPTH)]
        mx = functools.reduce(jnp.maximum, logits)
        es = [jnp.exp(t - mx) for t in logits]
        den = functools.reduce(lambda a, b: a + b, es)
        lb = jnp.zeros_like(den)
        for j in range(1, layer + 1):
            lb = lb + es[j] / den
        gs = []
        for h in range(HGRN_HEADS):
            hs = slice(h * HGRN_KDIM, (h + 1) * HGRN_KDIM)
            f = lb[:, hs] + (1.0 - lb[:, hs]) * jax.nn.sigmoid(f_ref[0, h])
            kk_ref[d, :, hs] = 1.0 - f
            gs.append(jnp.log(f) * LOG2_E)
        bcs_ref[d] =_dot_exact_lhs(jnp.where(tri, 1.0, 0.0).astype(BF), jnp.concatenate(gs, axis=1))

        end_row = (lambda b: b * blk + blk - 1) if d == 0 else (lambda b: b * blk)
        tot_row = q - 1 if d == 0 else 0
        pairs = [(b, b2) for b in range(nb) for b2 in range(nb) if (b2 < b if d == 0 else b2 > b)]
        for h in range(HGRN_HEADS):
            hs = slice(h * HGRN_KDIM, (h + 1) * HGRN_KDIM)
            qh = q_ref[0, h]
            vh = v_ref[0, h].astype(BF)
            kk = kk_ref[d, :, hs]
            bcs = bcs_ref[d, :, hs]
            s_t = state_ref[d, h]
            o = _dot_nt((qh * jnp.exp2(bcs)).astype(BF), s_t.astype(BF))
            r_end = [bcs_ref[d, end_row(b):end_row(b) + 1, hs] for b in range(nb)]
            kend = jnp.concatenate(
                [kk[b * blk:(b + 1) * blk] * jnp.exp2(r_end[b] - bcs[b * blk:(b + 1) * blk])
                 for b in range(nb)], axis=0)
            q_pairs = jnp.concatenate(
                [qh[b * blk:(b + 1) * blk] * jnp.exp2(bcs[b * blk:(b + 1) * blk] - r_end[b2])
                 for b, b2 in pairs], axis=0)
            r_off = _dot_nt(q_pairs.astype(BF), kend.astype(BF))
            p_rows = []
            for b in range(nb):
                bq = bcs[b * blk:(b + 1) * blk]
                qb = qh[b * blk:(b + 1) * blk]
                for j in range(blk):
                    s = b * blk + j
                    p_rows.append(qb * (kk_ref[d, s:s + 1, hs] * jnp.exp2(bq - bcs_ref[d, s:s + 1, hs])))
            r_diag = _dot(jnp.concatenate(p_rows, axis=0).astype(BF), ones_bf)
            blocks = []
            for b in range(nb):
                att_b = jnp.zeros((blk, q), F32)
                for n, (pb, b2) in enumerate(pairs):
                    if pb == b:
                        att_b = jnp.where(lane_blk == b2, r_off[n * blk:(n + 1) * blk], att_b)
                for j in range(blk):
                    s = b * blk + j
                    att_b = jnp.where(lane == s, r_diag[s * blk:(s + 1) * blk], att_b)
                blocks.append(att_b)
            att = jnp.where(tri, jnp.concatenate(blocks, axis=0), 0.0)
            o_refs[d][0, h] = o + _dot(att.astype(BF), vh)
            tot = bcs_ref[d, tot_row:tot_row + 1, hs]
            kdec = kk * jnp.exp2(tot - bcs)
            state_ref[d, h] = s_t * jnp.exp2(tot) + _dot_tn(vh, kdec.astype(BF))

    @pl.when(i == nc - 1)
    def _():
        for d in range(2):
            for h in range(HGRN_HEADS):
                st_ref[0, d, h] = state_ref[d, h].T


def _hgrn_scan(proj, h0, lb_logits, layer):
    bsz, _, L, _ = proj.shape
    q = HGRN_CHUNK
    nc = L // q
    nh = HGRN_HEADS

    def specs(cmap, d):
        return [
            pl.BlockSpec((1, nh, q, LANES), lambda b, i: (b, 0, cmap(i), 0)),
            pl.BlockSpec((1, nh, q, LANES), lambda b, i: (b, 1 + d, cmap(i), 0)),
            pl.BlockSpec((1, nh, q, LANES), lambda b, i: (b, 3, cmap(i), 0)),
        ]

    in_specs = specs(lambda i: i, 0) + specs(lambda i: nc - 1 - i, 1) + [_resident(lb_logits.shape)]
    args = [proj] * 6 + [lb_logits]
    has_init = h0 is not None
    st_block = (2, nh, HGRN_KDIM, HGRN_VDIM)
    if has_init:
        in_specs.append(pl.BlockSpec((1, 1) + st_block, lambda b, i: (b, 0, 0, 0, 0, 0)))
        args.append(h0)
    o_shape = jax.ShapeDtypeStruct((bsz, nh, L, LANES), F32)
    return pl.pallas_call(
        functools.partial(_hgrn_scan_kernel, nc=nc, layer=layer, has_init=has_init),
        out_shape=(o_shape, o_shape, jax.ShapeDtypeStruct((bsz,) + st_block, F32)),
        grid=(bsz, nc),
        in_specs=in_specs,
        out_specs=(
            pl.BlockSpec((1, nh, q, LANES), lambda b, i: (b, 0, i, 0)),
            pl.BlockSpec((1, nh, q, LANES), lambda b, i: (b, 0, nc - 1 - i, 0)),
            pl.BlockSpec((1,) + st_block, lambda b, i: (b, 0, 0, 0, 0)),
        ),
        scratch_shapes=[pltpu.VMEM((2, nh, HGRN_VDIM, HGRN_KDIM), F32),
                        pltpu.VMEM((2, q, HGRN_KW), F32), pltpu.VMEM((2, q, HGRN_KW), F32)],
        compiler_params=_cparams(2),
        name="hgrn_scan",
    )(*args)


def _hgrn_out_kernel(of_ref, ob_ref, g_ref, x_ref, mod_ref, ng_ref, wout_ref, lng_ref, lnb_ref,
                     o_ref, buf_ref):
    for c in range(HGRN_HEADS):
        o = of_ref[0, c] + ob_ref[0, c]
        r = lax.rsqrt(jnp.mean(o * o, axis=-1, keepdims=True) + LN_EPS)
        buf_ref[:, c * LANES:(c + 1) * LANES] = (o * r * ng_ref[...] * _silu(g_ref[0, c])).astype(BF)
    out = _dot(buf_ref[...], wout_ref[...])
    y = ALPHA * x_ref[0] + mod_ref[0, 2:3, :] * out
    o_ref[0] = _layer_norm(y, lng_ref[...], lnb_ref[...])


def _hgrn_out(of, ob, proj, x, mod, shared, norm_g, w_out, ln_g, ln_b):
    bsz, L, _ = x.shape
    tm = OUT_TM
    nh = HGRN_HEADS
    blk = pl.BlockSpec((1, nh, tm, LANES), lambda b, i: (b, 0, i, 0))
    return pl.pallas_call(
        _hgrn_out_kernel,
        out_shape=jax.ShapeDtypeStruct((bsz, L, D_MODEL), F32),
        grid=(bsz, L // tm),
        in_specs=[
            blk, blk,
            pl.BlockSpec((1, nh, tm, LANES), lambda b, i: (b, 4, i, 0)),
            pl.BlockSpec((1, tm, D_MODEL), lambda b, i: (b, i, 0)),
            pl.BlockSpec((1, 6, D_MODEL), _mod_index(shared)),
            _resident(norm_g.shape), _resident(w_out.shape),
            _resident(ln_g.shape), _resident(ln_b.shape),
        ],
        out_specs=pl.BlockSpec((1, tm, D_MODEL), lambda b, i: (b, i, 0)),
        scratch_shapes=[pltpu.VMEM((tm, HGRN_VW), BF)],
        compiler_params=_cparams(2),
        name="hgrn_out",
    )(of, ob, proj, x, mod, norm_g, w_out, ln_g, ln_b)


MOE_TM = 512


def _route(scores, sel):
    s = [scores[e:e + 1] for e in range(N_EXPERTS)]
    v = [sel[e:e + 1] for e in range(N_EXPERTS)]
    top2 = []
    for e in range(N_EXPERTS):
        g0 = (e // EXPERTS_PER_GROUP) * EXPERTS_PER_GROUP
        rank = None
        for j in range(g0, g0 + EXPERTS_PER_GROUP):
            if j == e:
                continue
            beats = jnp.where((v[j] >= v[e]) if j < e else (v[j] > v[e]), 1.0, 0.0)
            rank = beats if rank is None else rank + beats
        top2.append(rank < TOP_K)
    gscore = []
    for g in range(N_EXPERT_GROUPS):
        acc = None
        for e in range(g * EXPERTS_PER_GROUP, (g + 1) * EXPERTS_PER_GROUP):
            t = jnp.where(top2[e], v[e], 0.0)
            acc = t if acc is None else acc + t
        gscore.append(acc)
    best = []
    for g in range(N_EXPERT_GROUPS):
        ok = None
        for g2 in range(N_EXPERT_GROUPS):
            if g2 == g:
                continue
            c = (gscore[g2] < gscore[g]) if g2 < g else (gscore[g2] <= gscore[g])
            ok = c if ok is None else jnp.logical_and(ok, c)
        best.append(ok)
    chosen = [jnp.logical_and(best[e // EXPERTS_PER_GROUP], top2[e]) for e in range(N_EXPERTS)]
    den = None
    for e in range(N_EXPERTS):
        t = jnp.where(chosen[e], s[e], 0.0)
        den = t if den is None else den + t
    return jnp.concatenate([jnp.where(chosen[e], s[e] / den, 0.0) for e in range(N_EXPERTS)], axis=0)


def _moe_kernel(x_ref, mod_ref, wrh_ref, wrl_ref, br_ref, wgu_ref, wd_ref, lng_ref, lnb_ref,
                o_ref, hid_ref):
    x = x_ref[0]
    h = x * (1.0 + mod_ref[0, 4:5, :]) + mod_ref[0, 3:4, :]
    h_hi = h.astype(BF)
    h_lo = (h - h_hi.astype(F32)).astype(BF)
    logits = (_dot_nt(wrh_ref[...], h_hi) + _dot_nt(wrl_ref[...], h_hi)
              + _dot_nt(wrh_ref[...], h_lo))
    scores = jax.nn.sigmoid(logits)
    cw = _route(scores, scores + br_ref[...]).T
    for e in range(N_EXPERTS):
        gu = _dot(h_hi, wgu_ref[e])
        hid = _silu(gu[:, :D_EXPERT]) * gu[:, D_EXPERT:] * cw[:, e:e + 1]
        hid_ref[:, e * D_EXPERT:(e + 1) * D_EXPERT] = hid.astype(BF)
    out = _dot(hid_ref[...], wd_ref[...])
    y = ALPHA * x + mod_ref[0, 5:6, :] * out
    o_ref[0] = _layer_norm(y, lng_ref[...], lnb_ref[...])


def _moe_layer(x, mod, shared, wr_hi, wr_lo, b_r, w_gu, w_down, ln_g, ln_b):
    bsz, L, _ = x.shape
    tm = MOE_TM
    w_down = w_down.reshape(N_EXPERTS * D_EXPERT, D_MODEL)
    return pl.pallas_call(
        _moe_kernel,
        out_shape=jax.ShapeDtypeStruct((bsz, L, D_MODEL), F32),
        grid=(bsz, L // tm),
        in_specs=[
            pl.BlockSpec((1, tm, D_MODEL), lambda b, i: (b, i, 0)),
            pl.BlockSpec((1, 6, D_MODEL), _mod_index(shared)),
            _resident(wr_hi.shape), _resident(wr_lo.shape), _resident(b_r.shape),
            _resident(w_gu.shape), _resident(w_down.shape),
            _resident(ln_g.shape), _resident(ln_b.shape),
        ],
        out_specs=pl.BlockSpec((1, tm, D_MODEL), lambda b, i: (b, i, 0)),
        scratch_shapes=[pltpu.VMEM((tm, N_EXPERTS * D_EXPERT), BF)],
        compiler_params=_cparams(2),
        name="moe_layer",
    )(x, mod, wr_hi, wr_lo, b_r, w_gu, w_down, ln_g, ln_b)


def _grid_pos_embed(rows):
    quarter = D_MODEL // 4
    freq = jnp.exp(-math.log(10000.0) / quarter * jnp.arange(quarter, dtype=F32))
    r = jnp.repeat(jnp.arange(rows, dtype=F32), GRID_W)
    col = jnp.tile(jnp.arange(GRID_W, dtype=F32), rows)
    ar = r[:, None] * freq[None, :]
    ac = col[:, None] * freq[None, :]
    return jnp.concatenate([jnp.sin(ar), jnp.cos(ar), jnp.sin(ac), jnp.cos(ac)], axis=-1)


def _row(a):
    return a.reshape(1, -1)


def _col_blocks(a):
    n = a.shape[-1] // LANES
    return jnp.moveaxis(a.reshape(a.shape[:-1] + (n, LANES)), -2, 0)


def _prep_weights(p):
    w = {}
    w["gmlp_w_in"] = p["gmlp_w_in"].astype(BF)
    w["gmlp_w_s"] = p["gmlp_w_s"].astype(BF)
    w["gmlp_b_st"] = jnp.swapaxes(p["gmlp_b_s"], 1, 2)
    w["gmlp_w_out"] = p["gmlp_w_out"].astype(BF)
    wi = p["ssd_w_in"]
    n_l = wi.shape[0]
    pad = jnp.zeros((n_l, D_MODEL, LANES - 2 * SSD_HEADS), F32)
    w["ssd_w_in"] = jnp.concatenate([
        wi[:, :, SSD_INNER:2 * SSD_INNER],
        wi[:, :, :SSD_INNER],
        wi[:, :, 2 * SSD_INNER:SSD_INNER + SSD_CONV_DIM],
        wi[:, :, SSD_INNER + SSD_CONV_DIM:], pad], axis=-1).astype(BF)
    w["ssd_conv_w"] = jnp.stack([_col_blocks(t) for t in p["ssd_conv_w"]])
    w["ssd_conv_b"] = jnp.stack([_col_blocks(_row(t)) for t in p["ssd_conv_b"]])
    zpad = jnp.zeros((n_l, LANES - 2 * SSD_HEADS), F32)
    w["ssd_dt_bias"] = jnp.concatenate([p["ssd_dt_bias"].reshape(n_l, -1), zpad], axis=-1)[:, None, :]
    w["ssd_a_log"] = jnp.concatenate([p["ssd_a_log"].reshape(n_l, -1), zpad], axis=-1)[:, None, :]
    w["ssd_d_skip"] = jnp.stack([_col_blocks(_row(jnp.repeat(t, SSD_HEADDIM))) for t in p["ssd_d_skip"]])
    w["ssd_norm_g"] = jnp.stack([_col_blocks(_row(t)) for t in p["ssd_norm_g"]])
    w["ssd_w_out"] = p["ssd_w_out"].astype(BF)
    w["hgrn_w_in"] = p["hgrn_w_in"].astype(BF)
    w["hgrn_lb_logits"] = p["hgrn_lb_logits"].reshape(DEPTH, 2, 1, HGRN_KW)
    w["hgrn_w_out"] = p["hgrn_w_out"].astype(BF)
    wr_t = p["moe_w_router"].T
    wr_hi = wr_t.astype(BF)
    w["moe_wr_hi"] = wr_hi
    w["moe_wr_lo"] = (wr_t - wr_hi.astype(F32)).astype(BF)
    w["moe_b_r"] = p["moe_b_router"].reshape(N_EXPERTS, 1)
    w["moe_w_gu"] = p["moe_w_gu"].astype(BF)
    w["moe_w_down"] = p["moe_w_down"].astype(BF)
    return w


def _run_trunk(x, pos, mods, shared, ssd_init, hgrn_init, p, w):
    bsz, L, _ = x.shape
    flat = (1, bsz * L, D_MODEL) if shared else x.shape
    ssd_states, hgrn_states = [], []
    ia = ib = ic = 0
    for li in range(DEPTH):
        mod = mods[li]
        ln_g1, ln_b1 = _row(p["ln_g"][li, 0]), _row(p["ln_b"][li, 0])
        ln_g2, ln_b2 = _row(p["ln_g"][li, 1]), _row(p["ln_b"][li, 1])
        kind = li % N_MIXERS
        if kind == 0:
            x = _gmlp_layer(x.reshape(flat), pos if li == 0 else None, mod, shared,
                            w["gmlp_w_in"][ia], _row(p["gmlp_ln_g"][ia]), _row(p["gmlp_ln_b"][ia]),
                            w["gmlp_w_s"][ia], w["gmlp_b_st"][ia], w["gmlp_w_out"][ia],
                            ln_g1, ln_b1).reshape(bsz, L, D_MODEL)
            ia += 1
        elif kind == 1:
            proj = _inproj(x, mod, w["ssd_w_in"][ib], shared)
            h0 = None if ssd_init is None else ssd_init.reshape(
                bsz, -1, 2, SSD_XS_CB, LANES, SSD_STATE)[:, ib:ib + 1]
            yf, yb, st = _ssd_scan(proj, h0, w["ssd_conv_w"][ib], w["ssd_conv_b"][ib],
                                   w["ssd_dt_bias"][ib], w["ssd_a_log"][ib], w["ssd_d_skip"][ib])
            ssd_states.append(st.reshape(bsz, 2, SSD_HEADS, SSD_HEADDIM, SSD_STATE))
            x = _ssd_out(yf, yb, proj, x, mod, shared, w["ssd_norm_g"][ib], w["ssd_w_out"][ib],
                         ln_g1, ln_b1)
            ib += 1
        else:
            proj = _inproj(x, mod, w["hgrn_w_in"][ic], shared)
            h0 = None if hgrn_init is None else hgrn_init[:, ic:ic + 1]
            of, ob, st = _hgrn_scan(proj, h0, w["hgrn_lb_logits"], li)
            hgrn_states.append(st)
            x = _hgrn_out(of, ob, proj, x, mod, shared, _row(p["hgrn_norm_g"][ic]),
                          w["hgrn_w_out"][ic], ln_g1, ln_b1)
            ic += 1
        x = _moe_layer(x.reshape(flat), mod, shared, w["moe_wr_hi"], w["moe_wr_lo"], w["moe_b_r"],
                       w["moe_w_gu"][li], w["moe_w_down"][li], ln_g2, ln_b2).reshape(bsz, L, D_MODEL)
    return x, jnp.stack(ssd_states, axis=1), jnp.stack(hgrn_states, axis=1)


def kernel(x_prompt, x_sample, state_ssd, state_hgrn, c, c_ctx, w_mod, b_mod, ln_g, ln_b, gmlp_w_in, gmlp_ln_g, gmlp_ln_b, gmlp_w_s, gmlp_b_s, gmlp_w_out, ssd_w_in, ssd_conv_w, ssd_conv_b, ssd_dt_bias, ssd_a_log, ssd_d_skip, ssd_norm_g, ssd_w_out, hgrn_w_in, hgrn_lb_logits, hgrn_norm_g, hgrn_w_out, moe_w_router, moe_b_router, moe_w_gu, moe_w_down):
    p = dict(ln_g=ln_g, ln_b=ln_b, gmlp_w_in=gmlp_w_in, gmlp_ln_g=gmlp_ln_g, gmlp_ln_b=gmlp_ln_b,
             gmlp_w_s=gmlp_w_s, gmlp_b_s=gmlp_b_s, gmlp_w_out=gmlp_w_out, ssd_w_in=ssd_w_in,
             ssd_conv_w=ssd_conv_w, ssd_conv_b=ssd_conv_b, ssd_dt_bias=ssd_dt_bias,
             ssd_a_log=ssd_a_log, ssd_d_skip=ssd_d_skip, ssd_norm_g=ssd_norm_g, ssd_w_out=ssd_w_out,
             hgrn_w_in=hgrn_w_in, hgrn_lb_logits=hgrn_lb_logits, hgrn_norm_g=hgrn_norm_g,
             hgrn_w_out=hgrn_w_out, moe_w_router=moe_w_router, moe_b_router=moe_b_router,
             moe_w_gu=moe_w_gu, moe_w_down=moe_w_down)
    w = _prep_weights(p)
    n_lat = c.shape[0]
    cond = jnp.concatenate([c, c_ctx[None, :],
                            jnp.zeros((MOD_ROWS - n_lat - 1, D_MODEL), F32)], axis=0)
    mods = _modulation(cond, w_mod, b_mod).reshape(DEPTH, MOD_ROWS, 6, D_MODEL)
    mods_lat = [mods[li, :n_lat] for li in range(DEPTH)]
    mods_ctx = [mods[li, n_lat:n_lat + 1] for li in range(DEPTH)]
    y_prompt, new_ssd, new_hgrn = _run_trunk(x_prompt, None, mods_ctx, True, None, None, p, w)
    pos = _grid_pos_embed(x_sample.shape[1] // GRID_W)
    y_sample, _, _ = _run_trunk(x_sample, pos, mods_lat, False, state_ssd, state_hgrn, p, w)
    return (y_prompt, y_sample, new_ssd, new_hgrn)
```

```python
import functools
import math

import jax
import jax.numpy as jnp
from jax import lax
from jax.experimental import pallas as pl
from jax.experimental.pallas import tpu as pltpu

F32 = jnp.float32
BF = jnp.bfloat16

D_MODEL = 1024
DEPTH = 4
GRID_W = 64
N_MIXERS = 3
ALPHA = (2.0 * DEPTH) ** 0.25
LN_EPS = 1e-5

GMLP_CHUNK = 128
GMLP_WIDTH = 2 * D_MODEL
GMLP_GROUPS = 8
GMLP_GROUP_DIM = GMLP_WIDTH // GMLP_GROUPS

SSD_INNER = 2 * D_MODEL
SSD_HEADDIM = 64
SSD_HEADS = SSD_INNER // SSD_HEADDIM
SSD_GROUPS = 4
SSD_STATE = 128
SSD_CHUNK = 128
SSD_CONV_DIM = SSD_INNER + 2 * SSD_GROUPS * SSD_STATE

HGRN_HEADS = 8
HGRN_KDIM = 128
HGRN_VDIM = D_MODEL // HGRN_HEADS
HGRN_KW = HGRN_HEADS * HGRN_KDIM
HGRN_VW = HGRN_HEADS * HGRN_VDIM
HGRN_CHUNK = 64
HGRN_BLOCK = 8

N_EXPERTS = 16
N_EXPERT_GROUPS = 4
EXPERTS_PER_GROUP = N_EXPERTS // N_EXPERT_GROUPS
TOP_K = 2
D_EXPERT = 256

LANES = 128
BF16_ROWS = 16
VMEM_LIMIT_BYTES = 56 * 1024 * 1024
SQRT_HALF = 0.7071067811865476
LOG2_E = 1.4426950408889634

SSD_XS_CB = SSD_INNER // LANES
SSD_Z_CB0 = SSD_XS_CB
SSD_BC_CB0 = 2 * SSD_XS_CB
SSD_BC_CB = 2 * SSD_GROUPS * SSD_STATE // LANES
SSD_N16 = SSD_BC_CB0 + SSD_BC_CB
SSD_CONV_CB = SSD_XS_CB + SSD_BC_CB
SSD_WIDE_SPLIT = 512

HGRN_N16 = (HGRN_KW + 2 * HGRN_VW) // LANES


def _cparams(n_axes):
    return pltpu.CompilerParams(dimension_semantics=("arbitrary",) * n_axes,
                                vmem_limit_bytes=VMEM_LIMIT_BYTES)


def _resident(shape):
    nd = len(shape)
    return pl.BlockSpec(shape, lambda *_: (0,) * nd, pipeline_mode=pl.Buffered(1))


def _dot(a, b):
    return jnp.dot(a, b, preferred_element_type=F32)


def _dot_nt(a, b):
    return lax.dot_general(a, b, (((1,), (1,)), ((), ())), preferred_element_type=F32)


def _dot_tn(a, b):
    return lax.dot_general(a, b, (((0,), (0,)), ((), ())), preferred_element_type=F32)


def _split3(a):
    a0 = a.astype(BF)
    r1 = a - a0.astype(F32)
    a1 = r1.astype(BF)
    a2 = (r1 - a1.astype(F32)).astype(BF)
    return a0, a1, a2


def _dot_exact_lhs(m_bf, a):
    a0, a1, a2 = _split3(a)
    return _dot(m_bf, a0) + _dot(m_bf, a1) + _dot(m_bf, a2)


def _silu(t):
    return t * jax.nn.sigmoid(t)


def _gelu(t):
    return 0.5 * t * (1.0 + lax.erf(t * SQRT_HALF))


def _softplus(t):
    return jnp.maximum(t, 0.0) + jnp.log1p(jnp.exp(-jnp.abs(t)))


def _layer_norm(v, g, b):
    mu = jnp.mean(v, axis=-1, keepdims=True)
    c = v - mu
    var = jnp.mean(c * c, axis=-1, keepdims=True)
    return c * lax.rsqrt(var + LN_EPS) * g + b


def _mod_index(shared):
    return (lambda b, i: (0, 0, 0)) if shared else (lambda b, i: (b, 0, 0))


MOD_ROWS = 16
MOD_TN = 1536


def _mod_kernel(c_ref, w_ref, b_ref, o_ref):
    c = c_ref[...]
    s = _silu(c).astype(BF)
    o_ref[0] = _dot(s, w_ref[0].astype(BF)) + b_ref[0]


def _modulation(cond, w_mod, b_mod):
    n = 6 * D_MODEL
    return pl.pallas_call(
        _mod_kernel,
        out_shape=jax.ShapeDtypeStruct((DEPTH, MOD_ROWS, n), F32),
        grid=(DEPTH, n // MOD_TN),
        in_specs=[
            pl.BlockSpec((MOD_ROWS, D_MODEL), lambda l, j: (0, 0)),
            pl.BlockSpec((1, D_MODEL, MOD_TN), lambda l, j: (l, 0, j)),
            pl.BlockSpec((1, 1, MOD_TN), lambda l, j: (l, 0, j)),
        ],
        out_specs=pl.BlockSpec((1, MOD_ROWS, MOD_TN), lambda l, j: (l, 0, j)),
        compiler_params=_cparams(2),
        name="modulation",
    )(cond, w_mod, b_mod.reshape(DEPTH, 1, n))


INPROJ_TM = 256
INPROJ_NSPLIT = 8


def _inproj_kernel(x_ref, mod_ref, w_ref, o16_ref, o32_ref, *, n16, n32):
    x = x_ref[0]
    h = (x * (1.0 + mod_ref[0, 1:2, :]) + mod_ref[0, 0:1, :]).astype(BF)
    ncb = n16 + n32
    for c0 in range(0, ncb, INPROJ_NSPLIT):
        n = min(INPROJ_NSPLIT, ncb - c0)
        acc = _dot(h, w_ref[:, c0 * LANES:(c0 + n) * LANES])
        for c in range(n):
            blk = acc[:, c * LANES:(c + 1) * LANES]
            if c0 + c < n16:
                o16_ref[0, c0 + c] = blk.astype(BF)
            else:
                o32_ref[0, c0 + c - n16] = blk


def _inproj(x, mod, w, shared, n16):
    bsz, L, _ = x.shape
    n32 = w.shape[1] // LANES - n16
    tm = INPROJ_TM
    return pl.pallas_call(
        functools.partial(_inproj_kernel, n16=n16, n32=n32),
        out_shape=(jax.ShapeDtypeStruct((bsz, n16, L, LANES), BF),
                   jax.ShapeDtypeStruct((bsz, n32, L, LANES), F32)),
        grid=(bsz, L // tm),
        in_specs=[
            pl.BlockSpec((1, tm, D_MODEL), lambda b, i: (b, i, 0)),
            pl.BlockSpec((1, 6, D_MODEL), _mod_index(shared)),
            _resident(w.shape),
        ],
        out_specs=(pl.BlockSpec((1, n16, tm, LANES), lambda b, i: (b, 0, i, 0)),
                   pl.BlockSpec((1, n32, tm, LANES), lambda b, i: (b, 0, i, 0))),
        compiler_params=_cparams(2),
        name="inproj",
    )(x, mod, w)


GMLP_TM = 256
GMLP_NSPLIT = 512


def _gmlp_kernel(*refs, has_pos, tm):
    if has_pos:
        x_ref, pos_ref, *refs = refs
    else:
        x_ref, *refs = refs
    (mod_ref, win_ref, vg_ref, vb_ref, ws_ref, bst_ref, wout_ref, lng_ref, lnb_ref,
     o_ref, u_ref, v_ref, gated_ref) = refs
    x = x_ref[0]
    if has_pos:
        x = x + pos_ref[...]
    h = (x * (1.0 + mod_ref[0, 1:2, :]) + mod_ref[0, 0:1, :]).astype(BF)
    for c0 in range(0, GMLP_WIDTH, GMLP_NSPLIT):
        u_ref[:, c0:c0 + GMLP_NSPLIT] = _gelu(_dot(h, win_ref[:, c0:c0 + GMLP_NSPLIT]))
        v_ref[:, c0:c0 + GMLP_NSPLIT] = _gelu(
            _dot(h, win_ref[:, GMLP_WIDTH + c0:GMLP_WIDTH + c0 + GMLP_NSPLIT]))
    vn = _layer_norm(v_ref[...], vg_ref[...], vb_ref[...]).astype(BF)
    for c in range(tm // GMLP_CHUNK):
        r0 = c * GMLP_CHUNK
        for g in range(GMLP_GROUPS):
            c0 = g * GMLP_GROUP_DIM
            vv = vn[r0:r0 + GMLP_CHUNK, c0:c0 + GMLP_GROUP_DIM]
            mixed = _dot(ws_ref[g], vv) + bst_ref[:, g:g + 1]
            gated_ref[r0:r0 + GMLP_CHUNK, c0:c0 + GMLP_GROUP_DIM] = (
                u_ref[r0:r0 + GMLP_CHUNK, c0:c0 + GMLP_GROUP_DIM] * mixed).astype(BF)
    out = _dot(gated_ref[...], wout_ref[...])
    y = ALPHA * x + mod_ref[0, 2:3, :] * out
    o_ref[0] = _layer_norm(y, lng_ref[...], lnb_ref[...])


def _gmlp_layer(x, pos, mod, shared, w_in, v_g, v_b, w_s, b_st, w_out, ln_g, ln_b):
    bsz, L, _ = x.shape
    tm = GMLP_TM
    has_pos = pos is not None
    in_specs = [pl.BlockSpec((1, tm, D_MODEL), lambda b, i: (b, i, 0))]
    args = [x]
    if has_pos:
        in_specs.append(pl.BlockSpec((tm, D_MODEL), lambda b, i: (i, 0)))
        args.append(pos)
    in_specs += [
        pl.BlockSpec((1, 6, D_MODEL), _mod_index(shared)),
        _resident(w_in.shape), _resident(v_g.shape), _resident(v_b.shape),
        _resident(w_s.shape), _resident(b_st.shape), _resident(w_out.shape),
        _resident(ln_g.shape), _resident(ln_b.shape),
    ]
    args += [mod, w_in, v_g, v_b, w_s, b_st, w_out, ln_g, ln_b]
    return pl.pallas_call(
        functools.partial(_gmlp_kernel, has_pos=has_pos, tm=tm),
        out_shape=jax.ShapeDtypeStruct((bsz, L, D_MODEL), F32),
        grid=(bsz, L // tm),
        in_specs=in_specs,
        out_specs=pl.BlockSpec((1, tm, D_MODEL), lambda b, i: (b, i, 0)),
        scratch_shapes=[pltpu.VMEM((tm, GMLP_WIDTH), F32), pltpu.VMEM((tm, GMLP_WIDTH), F32),
                        pltpu.VMEM((tm, GMLP_WIDTH), BF)],
        compiler_params=_cparams(2),
        name="gmlp_layer",
    )(*args)


def _ssd_scan_kernel(*refs, nc, has_init):
    dirs = []
    for _ in range(2):
        dirs.append(refs[:7])
        refs = refs[7:]
    convw_ref, convb_ref, dtb_ref, alog_ref, dskip_ref, expand_ref, *refs = refs
    if has_init:
        h0_ref, *refs = refs
    yf_ref, yb_ref, st_ref, state_ref, act_ref, acs_ref, acst_ref, wide_ref, cb_ref = refs
    y_refs = (yf_ref, yb_ref)
    q = SSD_CHUNK
    i = pl.program_id(1)

    @pl.when(i == 0)
    def _():
        if has_init:
            state_ref[...] = h0_ref[0, 0]
        else:
            state_ref[...] = jnp.zeros(state_ref.shape, F32)

    row = lax.broadcasted_iota(jnp.int32, (q, q), 0)
    col = lax.broadcasted_iota(jnp.int32, (q, q), 1)
    lo = col < SSD_HEADDIM
    rowlo = row < SSD_HEADDIM
    a_neg = -jnp.exp(alog_ref[...]) * LOG2_E
    tris = [(row >= col), (row <= col)]
    chunk_decays = []

    for d in range(2):
        xs_ref, bc_ref, xsp_ref, bcp_ref, xsn_ref, bcn_ref, dt_ref = dirs[d]
        cidx = i if d == 0 else nc - 1 - i
        first = cidx == 0
        last = cidx == nc - 1
        for c in range(SSD_CONV_CB):
            if c < SSD_XS_CB:
                main, pr, nx = xs_ref[0, c], xsp_ref[0, c], xsn_ref[0, c]
            else:
                cc = c - SSD_XS_CB
                main, pr, nx = bc_ref[0, cc], bcp_ref[0, cc], bcn_ref[0, cc]
            main = main.astype(F32)
            pr = jnp.where(first, 0.0, pr.astype(F32)[BF16_ROWS - 1:BF16_ROWS, :])
            nx = jnp.where(last, 0.0, nx.astype(F32)[0:1, :])
            xp = jnp.where(row == 0, pr, pltpu.roll(main, 1, 0))
            xn = jnp.where(row == q - 1, nx, pltpu.roll(main, q - 1, 0))
            w = convw_ref[c]
            act_ref[d, c] = _silu(w[0:1, :] * xp + w[1:2, :] * main + w[2:3, :] * xn + convb_ref[c])

        dt = _softplus(dt_ref[0, 0] + dtb_ref[...])
        acs = _dot_exact_lhs(jnp.where(tris[d], 1.0, 0.0).astype(BF), dt * a_neg)
        acs_ref[d] = acs
        acst_ref[d] = acs.T
        tot = acs[q - 1:q, :] if d == 0 else acs[0:1, :]
        chunk_decays.append(jnp.exp2(tot))
        cols = jnp.concatenate([dt, jnp.exp2(acs), jnp.exp2(tot - acs) * dt], axis=0)
        c_hi = cols.astype(BF)
        c_lo = (cols - c_hi.astype(F32)).astype(BF)
        for c0 in range(0, SSD_INNER, SSD_WIDE_SPLIT):
            e_blk = expand_ref[d, :, c0:c0 + SSD_WIDE_SPLIT]
            wide_ref[d, :, c0:c0 + SSD_WIDE_SPLIT] = _dot(c_hi, e_blk) + _dot(c_lo, e_blk)
        for g in range(SSD_GROUPS):
            cb_ref[d, g] = _dot_nt(act_ref[d, SSD_XS_CB + SSD_GROUPS + g].astype(BF),
                                   act_ref[d, SSD_XS_CB + g].astype(BF))

    for d in range(2):
        tri = tris[d]
        chunk_decay = chunk_decays[d]
        for g in range(SSD_GROUPS):
            b_g = act_ref[d, SSD_XS_CB + g].astype(BF)
            c_g = act_ref[d, SSD_XS_CB + SSD_GROUPS + g].astype(BF)
            cb = cb_ref[d, g]
            for pp in range(SSD_XS_CB // SSD_GROUPS):
                p = g * (SSD_XS_CB // SSD_GROUPS) + pp
                k0 = d * SSD_HEADS + 2 * p
                k1 = k0 + 1
                xblk = act_ref[d, p]
                ps = slice(p * LANES, (p + 1) * LANES)
                ws = []
                for k in (k0, k1):
                    seg = acs_ref[d, :, k:k + 1] - acst_ref[d, k:k + 1, :]
                    ws.append((cb * jnp.exp2(jnp.where(tri, seg, -jnp.inf))).astype(BF))
                xdt = xblk * wide_ref[d, 0:q, ps]
                x_bd = jnp.concatenate([jnp.where(lo, xdt, 0.0).astype(BF),
                                        jnp.where(lo, 0.0, xdt).astype(BF)], axis=0)
                y = _dot(jnp.concatenate(ws, axis=1), x_bd)
                s_in = state_ref[d, p]
                y = y + _dot_nt(c_g, s_in.astype(BF)) * wide_ref[d, q:2 * q, ps]
                if d == 0:
                    y = y + xblk * dskip_ref[p]
                y_refs[d][0, p] = y.astype(BF)
                xw =(xblk * wide_ref[d, 2 * q:3 * q, ps]).astype(BF)
                state_ref[d, p] = s_in * jnp.where(
                    rowlo, chunk_decay[:, k0:k0 + 1], chunk_decay[:, k1:k1 + 1]) + _dot_tn(xw, b_g)

    @pl.when(i == nc - 1)
    def _():
        st_ref[0] = state_ref[...]


def _ssd_expand_matrix():
    k = jnp.arange(LANES)[None, :, None]
    head = (jnp.arange(SSD_INNER) // SSD_HEADDIM)[None, None, :]
    d = jnp.arange(2)[:, None, None]
    return (k == d * SSD_HEADS + head).astype(BF)


def _ssd_scan(proj16, proj32, h0, convw, convb, dtb, alog, dskip):
    bsz, _, L, _ = proj16.shape
    q = SSD_CHUNK
    nc = L // q
    halo = BF16_ROWS
    halo_max = L // halo - 1
    hpc = q // halo

    def specs(cmap):
        prev = lambda b, i: jnp.maximum(cmap(i) * hpc - 1, 0)
        nxt = lambda b, i: jnp.minimum(cmap(i) * hpc + hpc, halo_max)
        bc0 = SSD_BC_CB0 // SSD_BC_CB
        return [
            pl.BlockSpec((1, SSD_XS_CB, q, LANES), lambda b, i: (b, 0, cmap(i), 0)),
            pl.BlockSpec((1, SSD_BC_CB, q, LANES), lambda b, i: (b, bc0, cmap(i), 0)),
            pl.BlockSpec((1, SSD_XS_CB, halo, LANES), lambda b, i: (b, 0, prev(b, i), 0)),
            pl.BlockSpec((1, SSD_BC_CB, halo, LANES), lambda b, i: (b, bc0, prev(b, i), 0)),
            pl.BlockSpec((1, SSD_XS_CB, halo, LANES), lambda b, i: (b, 0, nxt(b, i), 0)),
            pl.BlockSpec((1, SSD_BC_CB, halo, LANES), lambda b, i: (b, bc0, nxt(b, i), 0)),
            pl.BlockSpec((1, 1, q, LANES), lambda b, i: (b, 0, cmap(i), 0)),
        ]

    in_specs = specs(lambda i: i) + specs(lambda i: nc - 1 - i)
    args = ([proj16] * 6 + [proj32]) * 2
    for a in (convw, convb, dtb, alog, dskip, _ssd_expand_matrix()):
        in_specs.append(_resident(a.shape))
        args.append(a)
    has_init = h0 is not None
    st_block = (2, SSD_XS_CB, LANES, SSD_STATE)
    if has_init:
        in_specs.append(pl.BlockSpec((1, 1) + st_block, lambda b, i: (b, 0, 0, 0, 0, 0)))
        args.append(h0)
    y_shape = jax.ShapeDtypeStruct((bsz, SSD_XS_CB, L, LANES), BF)
    return pl.pallas_call(
        functools.partial(_ssd_scan_kernel, nc=nc, has_init=has_init),
        out_shape=(y_shape, y_shape, jax.ShapeDtypeStruct((bsz,) + st_block, F32)),
        grid=(bsz, nc),
        in_specs=in_specs,
        out_specs=(
            pl.BlockSpec((1, SSD_XS_CB, q, LANES), lambda b, i: (b, 0, i, 0)),
            pl.BlockSpec((1, SSD_XS_CB, q, LANES), lambda b, i: (b, 0, nc - 1 - i, 0)),
            pl.BlockSpec((1,) + st_block, lambda b, i: (b, 0, 0, 0, 0)),
        ),
        scratch_shapes=[pltpu.VMEM(st_block, F32), pltpu.VMEM((2, SSD_CONV_CB, q, LANES), F32),
                        pltpu.VMEM((2, q, LANES), F32), pltpu.VMEM((2, q, LANES), F32),
                        pltpu.VMEM((2, 3 * q, SSD_INNER), F32),
                        pltpu.VMEM((2, SSD_GROUPS, q, q), F32)],
        compiler_params=_cparams(2),
        name="ssd_scan",
    )(*args)


OUT_TM = 256


def _ssd_out_kernel(yf_ref, yb_ref, z_ref, x_ref, mod_ref, ng_ref, wout_ref, lng_ref, lnb_ref,
                    o_ref, yz_ref, buf_ref):
    ssq = None
    for c in range(SSD_XS_CB):
        yz = (yf_ref[0, c].astype(F32) + yb_ref[0, c].astype(F32)) * _silu(z_ref[0, c].astype(F32))
        yz_ref[c] = yz
        ssq = yz * yz if ssq is None else ssq + yz * yz
    r = lax.rsqrt(jnp.sum(ssq, axis=-1, keepdims=True) * (1.0 / SSD_INNER) + LN_EPS)
    for c in range(SSD_XS_CB):
        buf_ref[:, c * LANES:(c + 1) * LANES] = (yz_ref[c] * r * ng_ref[c]).astype(BF)
    out = _dot(buf_ref[...], wout_ref[...])
    y = ALPHA * x_ref[0] + mod_ref[0, 2:3, :] * out
    o_ref[0] = _layer_norm(y, lng_ref[...], lnb_ref[...])


def _ssd_out(yf, yb, proj, x, mod, shared, norm_g, w_out, ln_g, ln_b):
    bsz, L, _ = x.shape
    tm = OUT_TM
    blk = pl.BlockSpec((1, SSD_XS_CB, tm, LANES), lambda b, i: (b, 0, i, 0))
    return pl.pallas_call(
        _ssd_out_kernel,
        out_shape=jax.ShapeDtypeStruct((bsz, L, D_MODEL), F32),
        grid=(bsz, L // tm),
        in_specs=[
            blk, blk,
            pl.BlockSpec((1, SSD_XS_CB, tm, LANES), lambda b, i: (b, SSD_Z_CB0 // SSD_XS_CB, i, 0)),
            pl.BlockSpec((1, tm, D_MODEL), lambda b, i: (b, i, 0)),
            pl.BlockSpec((1, 6, D_MODEL), _mod_index(shared)),
            _resident(norm_g.shape), _resident(w_out.shape),
            _resident(ln_g.shape), _resident(ln_b.shape),
        ],
        out_specs=pl.BlockSpec((1, tm, D_MODEL), lambda b, i: (b, i, 0)),
        scratch_shapes=[pltpu.VMEM((SSD_XS_CB, tm, LANES), F32), pltpu.VMEM((tm, SSD_INNER), BF)],
        compiler_params=_cparams(2),
        name="ssd_out",
    )(yf, yb, proj, x, mod, norm_g, w_out, ln_g, ln_b)


def _hgrn_scan_kernel(*refs, nc, layer, has_init):
    dirs = (refs[0:3], refs[3:6])
    lbl_ref, *refs = refs[6:]
    if has_init:
        h0_ref, *refs = refs
    of_ref, ob_ref, st_ref, state_ref, kk_ref, bcs_ref, roff_ref, rdiag_ref, ointer_ref = refs
    o_refs = (of_ref, ob_ref)
    q = HGRN_CHUNK
    blk = HGRN_BLOCK
    nb = q // blk
    i = pl.program_id(1)

    @pl.when(i == 0)
    def _():
        for d in range(2):
            for h in range(HGRN_HEADS):
                if has_init:
                    state_ref[d, h] = h0_ref[0, 0, d, h].T
                else:
                    state_ref[d, h] = jnp.zeros((HGRN_VDIM, HGRN_KDIM), F32)

    row = lax.broadcasted_iota(jnp.int32, (q, q), 0)
    col = lax.broadcasted_iota(jnp.int32, (q, q), 1)
    lane = lax.broadcasted_iota(jnp.int32, (blk, q), 1)
    lane_blk = lane // blk
    ones_bf = jnp.ones((HGRN_KDIM, q), BF)

    tris = [(row >= col), (row <= col)]
    all_pairs = [[(b, b2) for b in range(nb) for b2 in range(nb) if (b2 < b if d == 0 else b2 > b)]
                 for d in range(2)]

    for d in range(2):
        q_ref, f_ref, v_ref = dirs[d]
        tri = tris[d]
        logits = [lbl_ref[j, d] for j in range(DEPTH)]
        mx = functools.reduce(jnp.maximum, logits)
        es = [jnp.exp(t - mx) for t in logits]
        den = functools.reduce(lambda a, b: a + b, es)
        lb = jnp.zeros_like(den)
        for j in range(1, layer + 1):
            lb = lb + es[j] / den
        gs = []
        for h in range(HGRN_HEADS):
            hs = slice(h * HGRN_KDIM, (h + 1) * HGRN_KDIM)
            f = lb[:, hs] + (1.0 - lb[:, hs]) * jax.nn.sigmoid(f_ref[0, h])
            kk_ref[d, :, hs] = 1.0 - f
            gs.append(jnp.log(f) * LOG2_E)
        bcs_ref[d] = _dot_exact_lhs(jnp.where(tri, 1.0, 0.0).astype(BF), jnp.concatenate(gs, axis=1))

    for d in range(2):
        q_ref, f_ref, v_ref = dirs[d]
        end_row = (lambda b: b * blk + blk - 1) if d == 0 else (lambda b: b * blk)
        tot_row = q - 1 if d == 0 else 0
        pairs = all_pairs[d]
        for h in range(HGRN_HEADS):
            hs = slice(h * HGRN_KDIM, (h + 1) * HGRN_KDIM)
            qh = q_ref[0, h].astype(F32)
            vh = v_ref[0, h]
            kk = kk_ref[d, :, hs]
            bcs = bcs_ref[d, :, hs]
            s_t = state_ref[d, h]
            ointer_ref[d, h] = _dot_nt((qh * jnp.exp2(bcs)).astype(BF), s_t.astype(BF))
            tot = bcs_ref[d, tot_row:tot_row + 1, hs]
            kdec = kk * jnp.exp2(tot - bcs)
            state_ref[d, h] = s_t * jnp.exp2(tot) + _dot_tn(vh, kdec.astype(BF))
            r_end = [bcs_ref[d, end_row(b):end_row(b) + 1, hs] for b in range(nb)]
            kend = jnp.concatenate(
                [kk[b * blk:(b + 1) * blk] * jnp.exp2(r_end[b] - bcs[b * blk:(b + 1) * blk])
                 for b in range(nb)], axis=0)
            q_pairs = jnp.concatenate(
                [qh[b * blk:(b + 1) * blk] * jnp.exp2(bcs[b * blk:(b + 1) * blk] - r_end[b2])
                 for b, b2 in pairs], axis=0)
            roff_ref[d, h] = _dot_nt(q_pairs.astype(BF), kend.astype(BF))
            p_rows = []
            for b in range(nb):
                bq = bcs[b * blk:(b + 1) * blk]
                qb = qh[b * blk:(b + 1) * blk]
                for j in range(blk):
                    s = b * blk + j
                    p_rows.append(qb * (kk_ref[d, s:s + 1, hs] * jnp.exp2(bq - bcs_ref[d, s:s + 1, hs])))
            rdiag_ref[d, h] = _dot(jnp.concatenate(p_rows, axis=0).astype(BF), ones_bf)

    for d in range(2):
        v_ref = dirs[d][2]
        pairs = all_pairs[d]
        for h in range(HGRN_HEADS):
            blocks = []
            for b in range(nb):
                att_b = jnp.zeros((blk, q), F32)
                for n, (pb, b2) in enumerate(pairs):
                    if pb == b:
                        att_b = jnp.where(lane_blk == b2, roff_ref[d, h, n * blk:(n + 1) * blk], att_b)
                for j in range(blk):
                    s = b * blk + j
                    att_b = jnp.where(lane == s, rdiag_ref[d, h, s * blk:(s + 1) * blk], att_b)
                blocks.append(att_b)
            att = jnp.where(tris[d], jnp.concatenate(blocks, axis=0), 0.0)
            o_refs[d][0, h] = (ointer_ref[d, h] + _dot(att.astype(BF), v_ref[0, h])).astype(BF)

    @pl.when(i == nc - 1)
    def _():
        for d in range(2):
            for h in range(HGRN_HEADS):
                st_ref[0, d, h] = state_ref[d, h].T


def _hgrn_scan(proj16, proj32, h0, lb_logits, layer):
    bsz, _, L, _ = proj16.shape
    q = HGRN_CHUNK
    nc = L // q
    nh = HGRN_HEADS

    def specs(cmap, d):
        return [
            pl.BlockSpec((1, nh, q, LANES), lambda b, i: (b, 0, cmap(i), 0)),
            pl.BlockSpec((1, nh, q, LANES), lambda b, i: (b, d, cmap(i), 0)),
            pl.BlockSpec((1, nh, q, LANES), lambda b, i: (b, 1, cmap(i), 0)),
        ]

    in_specs = specs(lambda i: i, 0) + specs(lambda i: nc - 1 - i, 1) + [_resident(lb_logits.shape)]
    args = [proj16, proj32, proj16] * 2 + [lb_logits]
    has_init = h0 is not None
    st_block = (2, nh, HGRN_KDIM, HGRN_VDIM)
    if has_init:
        in_specs.append(pl.BlockSpec((1, 1) + st_block, lambda b, i: (b, 0, 0, 0, 0, 0)))
        args.append(h0)
    o_shape = jax.ShapeDtypeStruct((bsz, nh, L, LANES), BF)
    n_blocks = q // HGRN_BLOCK
    n_pairs = n_blocks * (n_blocks - 1) // 2
    return pl.pallas_call(
        functools.partial(_hgrn_scan_kernel, nc=nc, layer=layer, has_init=has_init),
        out_shape=(o_shape, o_shape, jax.ShapeDtypeStruct((bsz,) + st_block, F32)),
        grid=(bsz, nc),
        in_specs=in_specs,
        out_specs=(
            pl.BlockSpec((1, nh, q, LANES), lambda b, i: (b, 0, i, 0)),
            pl.BlockSpec((1, nh, q, LANES), lambda b, i: (b, 0, nc - 1 - i, 0)),
            pl.BlockSpec((1,) + st_block, lambda b, i: (b, 0, 0, 0, 0)),
        ),
        scratch_shapes=[pltpu.VMEM((2, nh, HGRN_VDIM, HGRN_KDIM), F32),
                        pltpu.VMEM((2, q, HGRN_KW), F32), pltpu.VMEM((2, q, HGRN_KW), F32),
                        pltpu.VMEM((2, nh, n_pairs * HGRN_BLOCK, q), F32),
                        pltpu.VMEM((2, nh, q * HGRN_BLOCK, q), F32),
                        pltpu.VMEM((2, nh, q, HGRN_VDIM), F32)],
        compiler_params=_cparams(2),
        name="hgrn_scan",
    )(*args)


def _hgrn_out_kernel(of_ref, ob_ref, g_ref, x_ref, mod_ref, ng_ref, wout_ref, lng_ref, lnb_ref,
                     o_ref, buf_ref):
    for c in range(HGRN_HEADS):
        o = of_ref[0, c].astype(F32) + ob_ref[0, c].astype(F32)
        r = lax.rsqrt(jnp.mean(o * o, axis=-1, keepdims=True) + LN_EPS)
        buf_ref[:, c * LANES:(c + 1) * LANES] = (
            o * r * ng_ref[...] * _silu(g_ref[0, c].astype(F32))).astype(BF)
    out = _dot(buf_ref[...], wout_ref[...])
    y = ALPHA * x_ref[0] + mod_ref[0, 2:3, :] * out
    o_ref[0] = _layer_norm(y, lng_ref[...], lnb_ref[...])


def _hgrn_out(of, ob, proj, x, mod, shared, norm_g, w_out, ln_g, ln_b):
    bsz, L, _ = x.shape
    tm = OUT_TM
    nh = HGRN_HEADS
    blk = pl.BlockSpec((1, nh, tm, LANES), lambda b, i: (b, 0, i, 0))
    return pl.pallas_call(
        _hgrn_out_kernel,
        out_shape=jax.ShapeDtypeStruct((bsz, L, D_MODEL), F32),
        grid=(bsz, L // tm),
        in_specs=[
            blk, blk,
            pl.BlockSpec((1, nh, tm, LANES), lambda b, i: (b, 2, i, 0)),
            pl.BlockSpec((1, tm, D_MODEL), lambda b, i: (b, i, 0)),
            pl.BlockSpec((1, 6, D_MODEL), _mod_index(shared)),
            _resident(norm_g.shape), _resident(w_out.shape),
            _resident(ln_g.shape), _resident(ln_b.shape),
        ],
        out_specs=pl.BlockSpec((1, tm, D_MODEL), lambda b, i: (b, i, 0)),
        scratch_shapes=[pltpu.VMEM((tm, HGRN_VW), BF)],
        compiler_params=_cparams(2),
        name="hgrn_out",
    )(of, ob, proj, x, mod, norm_g, w_out, ln_g, ln_b)


MOE_TM = 512


def _route(scores, sel):
    s = [scores[e:e + 1] for e in range(N_EXPERTS)]
    v = [sel[e:e + 1] for e in range(N_EXPERTS)]
    top2 = []
    for e in range(N_EXPERTS):
        g0 = (e // EXPERTS_PER_GROUP) * EXPERTS_PER_GROUP
        rank = None
        for j in range(g0, g0 + EXPERTS_PER_GROUP):
            if j == e:
                continue
            beats = jnp.where((v[j] >= v[e]) if j < e else (v[j] > v[e]), 1.0, 0.0)
            rank = beats if rank is None else rank + beats
        top2.append(rank < TOP_K)
    gscore = []
    for g in range(N_EXPERT_GROUPS):
        acc = None
        for e in range(g * EXPERTS_PER_GROUP, (g + 1) * EXPERTS_PER_GROUP):
            t = jnp.where(top2[e], v[e], 0.0)
            acc = t if acc is None else acc + t
        gscore.append(acc)
    best = []
    for g in range(N_EXPERT_GROUPS):
        ok = None
        for g2 in range(N_EXPERT_GROUPS):
            if g2 == g:
                continue
            c = (gscore[g2] < gscore[g]) if g2 < g else (gscore[g2] <= gscore[g])
            ok = c if ok is None else jnp.logical_and(ok, c)
        best.append(ok)
    chosen = [jnp.logical_and(best[e // EXPERTS_PER_GROUP], top2[e]) for e in range(N_EXPERTS)]
    den = None
    for e in range(N_EXPERTS):
        t = jnp.where(chosen[e], s[e], 0.0)
        den = t if den is None else den + t
    return jnp.concatenate([jnp.where(chosen[e], s[e] / den, 0.0) for e in range(N_EXPERTS)], axis=0)


def _moe_kernel(x_ref, mod_ref, wrh_ref, wrl_ref, br_ref, wgu_ref, wd_ref, lng_ref, lnb_ref,
                o_ref, hid_ref):
    x = x_ref[0]
    h = x * (1.0 + mod_ref[0, 4:5, :]) + mod_ref[0, 3:4, :]
    h_hi = h.astype(BF)
    h_lo = (h - h_hi.astype(F32)).astype(BF)
    logits = (_dot_nt(wrh_ref[...], h_hi) + _dot_nt(wrl_ref[...], h_hi)
              + _dot_nt(wrh_ref[...], h_lo))
    scores = jax.nn.sigmoid(logits)
    cw = _route(scores, scores + br_ref[...]).T
    for e in range(N_EXPERTS):
        gu = _dot(h_hi, wgu_ref[e])
        hid = _silu(gu[:, :D_EXPERT]) * gu[:, D_EXPERT:] * cw[:, e:e + 1]
        hid_ref[:, e * D_EXPERT:(e + 1) * D_EXPERT] = hid.astype(BF)
    out = _dot(hid_ref[...], wd_ref[...])
    y = ALPHA * x + mod_ref[0, 5:6, :] * out
    o_ref[0] = _layer_norm(y, lng_ref[...], lnb_ref[...])


def _moe_layer(x, mod, shared, wr_hi, wr_lo, b_r, w_gu, w_down, ln_g, ln_b):
    bsz, L, _ = x.shape
    tm = MOE_TM
    w_down = w_down.reshape(N_EXPERTS * D_EXPERT, D_MODEL)
    return pl.pallas_call(
        _moe_kernel,
        out_shape=jax.ShapeDtypeStruct((bsz, L, D_MODEL), F32),
        grid=(bsz, L // tm),
        in_specs=[
            pl.BlockSpec((1, tm, D_MODEL), lambda b, i: (b, i, 0)),
            pl.BlockSpec((1, 6, D_MODEL), _mod_index(shared)),
            _resident(wr_hi.shape), _resident(wr_lo.shape), _resident(b_r.shape),
            _resident(w_gu.shape), _resident(w_down.shape),
            _resident(ln_g.shape), _resident(ln_b.shape),
        ],
        out_specs=pl.BlockSpec((1, tm, D_MODEL), lambda b, i: (b, i, 0)),
        scratch_shapes=[pltpu.VMEM((tm, N_EXPERTS * D_EXPERT), BF)],
        compiler_params=_cparams(2),
        name="moe_layer",
    )(x, mod, wr_hi, wr_lo, b_r, w_gu, w_down, ln_g, ln_b)


def _grid_pos_embed(rows):
    quarter = D_MODEL // 4
    freq = jnp.exp(-math.log(10000.0) / quarter * jnp.arange(quarter, dtype=F32))
    r = jnp.repeat(jnp.arange(rows, dtype=F32), GRID_W)
    col = jnp.tile(jnp.arange(GRID_W, dtype=F32), rows)
    ar = r[:, None] * freq[None, :]
    ac = col[:, None] * freq[None, :]
    return jnp.concatenate([jnp.sin(ar), jnp.cos(ar), jnp.sin(ac), jnp.cos(ac)], axis=-1)


def _row(a):
    return a.reshape(1, -1)


def _col_blocks(a):
    n = a.shape[-1] // LANES
    return jnp.moveaxis(a.reshape(a.shape[:-1] + (n, LANES)), -2, 0)


def _prep_weights(p):
    w = {}
    w["gmlp_w_in"] = p["gmlp_w_in"].astype(BF)
    w["gmlp_w_s"] = p["gmlp_w_s"].astype(BF)
    w["gmlp_b_st"] = jnp.swapaxes(p["gmlp_b_s"], 1, 2)
    w["gmlp_w_out"] = p["gmlp_w_out"].astype(BF)
    wi = p["ssd_w_in"]
    n_l = wi.shape[0]
    pad = jnp.zeros((n_l, D_MODEL, LANES - 2 * SSD_HEADS), F32)
    w["ssd_w_in"] = jnp.concatenate([
        wi[:, :, SSD_INNER:2 * SSD_INNER],
        wi[:, :, :SSD_INNER],
        wi[:, :, 2 * SSD_INNER:SSD_INNER + SSD_CONV_DIM],
        wi[:, :, SSD_INNER + SSD_CONV_DIM:], pad], axis=-1).astype(BF)
    w["ssd_conv_w"] = jnp.stack([_col_blocks(t) for t in p["ssd_conv_w"]])
    w["ssd_conv_b"] = jnp.stack([_col_blocks(_row(t)) for t in p["ssd_conv_b"]])
    zpad = jnp.zeros((n_l, LANES - 2 * SSD_HEADS), F32)
    w["ssd_dt_bias"] = jnp.concatenate([p["ssd_dt_bias"].reshape(n_l, -1), zpad], axis=-1)[:, None, :]
    w["ssd_a_log"] = jnp.concatenate([p["ssd_a_log"].reshape(n_l, -1), zpad], axis=-1)[:, None, :]
    w["ssd_d_skip"] = jnp.stack([_col_blocks(_row(jnp.repeat(t, SSD_HEADDIM))) for t in p["ssd_d_skip"]])
    w["ssd_norm_g"] = jnp.stack([_col_blocks(_row(t)) for t in p["ssd_norm_g"]])
    w["ssd_w_out"] = p["ssd_w_out"].astype(BF)
    hw = p["hgrn_w_in"]
    w["hgrn_w_in"] = jnp.concatenate([
        hw[:, :, :HGRN_KW],
        hw[:, :, 3 * HGRN_KW:],
        hw[:, :, HGRN_KW:3 * HGRN_KW]], axis=-1).astype(BF)
    w["hgrn_lb_logits"] = p["hgrn_lb_logits"].reshape(DEPTH, 2, 1, HGRN_KW)
    w["hgrn_w_out"] = p["hgrn_w_out"].astype(BF)
    wr_t = p["moe_w_router"].T
    wr_hi = wr_t.astype(BF)
    w["moe_wr_hi"] = wr_hi
    w["moe_wr_lo"] = (wr_t - wr_hi.astype(F32)).astype(BF)
    w["moe_b_r"] = p["moe_b_router"].reshape(N_EXPERTS, 1)
    w["moe_w_gu"] = p["moe_w_gu"].astype(BF)
    w["moe_w_down"] = p["moe_w_down"].astype(BF)
    return w


def _run_trunk(x, pos, mods, shared, ssd_init, hgrn_init, p, w):
    bsz, L, _ = x.shape
    flat = (1, bsz * L, D_MODEL) if shared else x.shape
    ssd_states, hgrn_states = [], []
    ia = ib = ic = 0
    for li in range(DEPTH):
        mod = mods[li]
        ln_g1, ln_b1 = _row(p["ln_g"][li, 0]), _row(p["ln_b"][li, 0])
        ln_g2, ln_b2 = _row(p["ln_g"][li, 1]), _row(p["ln_b"][li, 1])
        kind = li % N_MIXERS
        if kind == 0:
            x = _gmlp_layer(x.reshape(flat), pos if li == 0 else None, mod, shared,
                            w["gmlp_w_in"][ia], _row(p["gmlp_ln_g"][ia]), _row(p["gmlp_ln_b"][ia]),
                            w["gmlp_w_s"][ia], w["gmlp_b_st"][ia], w["gmlp_w_out"][ia],
                            ln_g1, ln_b1).reshape(bsz, L, D_MODEL)
            ia += 1
        elif kind == 1:
            proj16, proj32 = _inproj(x, mod, w["ssd_w_in"][ib], shared, SSD_N16)
            h0 = None if ssd_init is None else ssd_init.reshape(
                bsz, -1, 2, SSD_XS_CB, LANES, SSD_STATE)[:, ib:ib + 1]
            yf, yb, st = _ssd_scan(proj16, proj32, h0, w["ssd_conv_w"][ib], w["ssd_conv_b"][ib],
                                   w["ssd_dt_bias"][ib], w["ssd_a_log"][ib], w["ssd_d_skip"][ib])
            ssd_states.append(st.reshape(bsz, 2, SSD_HEADS, SSD_HEADDIM, SSD_STATE))
            x = _ssd_out(yf, yb, proj16, x, mod, shared, w["ssd_norm_g"][ib], w["ssd_w_out"][ib],
                         ln_g1, ln_b1)
            ib += 1
        else:
            proj16, proj32 = _inproj(x, mod, w["hgrn_w_in"][ic], shared, HGRN_N16)
            h0 = None if hgrn_init is None else hgrn_init[:, ic:ic + 1]
            of, ob, st = _hgrn_scan(proj16, proj32, h0, w["hgrn_lb_logits"], li)
            hgrn_states.append(st)
            x = _hgrn_out(of, ob, proj16, x, mod, shared, _row(p["hgrn_norm_g"][ic]),
                          w["hgrn_w_out"][ic], ln_g1, ln_b1)
            ic += 1
        x = _moe_layer(x.reshape(flat), mod, shared, w["moe_wr_hi"], w["moe_wr_lo"], w["moe_b_r"],
                       w["moe_w_gu"][li], w["moe_w_down"][li], ln_g2, ln_b2).reshape(bsz, L, D_MODEL)
    return x, jnp.stack(ssd_states, axis=1), jnp.stack(hgrn_states, axis=1)


def kernel(x_prompt, x_sample, state_ssd, state_hgrn, c, c_ctx, w_mod, b_mod, ln_g, ln_b, gmlp_w_in, gmlp_ln_g, gmlp_ln_b, gmlp_w_s, gmlp_b_s, gmlp_w_out, ssd_w_in, ssd_conv_w, ssd_conv_b, ssd_dt_bias, ssd_a_log, ssd_d_skip, ssd_norm_g, ssd_w_out, hgrn_w_in, hgrn_lb_logits, hgrn_norm_g, hgrn_w_out, moe_w_router, moe_b_router, moe_w_gu, moe_w_down):
    p = dict(ln_g=ln_g, ln_b=ln_b, gmlp_w_in=gmlp_w_in, gmlp_ln_g=gmlp_ln_g, gmlp_ln_b=gmlp_ln_b,
             gmlp_w_s=gmlp_w_s, gmlp_b_s=gmlp_b_s, gmlp_w_out=gmlp_w_out, ssd_w_in=ssd_w_in,
             ssd_conv_w=ssd_conv_w, ssd_conv_b=ssd_conv_b, ssd_dt_bias=ssd_dt_bias,
             ssd_a_log=ssd_a_log, ssd_d_skip=ssd_d_skip, ssd_norm_g=ssd_norm_g, ssd_w_out=ssd_w_out,
             hgrn_w_in=hgrn_w_in, hgrn_lb_logits=hgrn_lb_logits, hgrn_norm_g=hgrn_norm_g,
             hgrn_w_out=hgrn_w_out, moe_w_router=moe_w_router, moe_b_router=moe_b_router,
             moe_w_gu=moe_w_gu, moe_w_down=moe_w_down)
    w = _prep_weights(p)
    n_lat = c.shape[0]
    cond = jnp.concatenate([c, c_ctx[None, :],
                            jnp.zeros((MOD_ROWS - n_lat - 1, D_MODEL), F32)], axis=0)
    mods = _modulation(cond, w_mod, b_mod).reshape(DEPTH, MOD_ROWS, 6, D_MODEL)
    mods_lat = [mods[li, :n_lat] for li in range(DEPTH)]
    mods_ctx = [mods[li, n_lat:n_lat + 1] for li in range(DEPTH)]
    y_prompt, new_ssd, new_hgrn = _run_trunk(x_prompt, None, mods_ctx, True, None, None, p, w)
    pos = _grid_pos_embed(x_sample.shape[1] // GRID_W)
    y_sample, _, _ = _run_trunk(x_sample, pos, mods_lat, False, state_ssd, state_hgrn, p, w)
    return (y_prompt, y_sample, new_ssd, new_hgrn)
```

```python
import functools
import math

import jax
import jax.numpy as jnp
from jax import lax
from jax.experimental import pallas as pl
from jax.experimental.pallas import tpu as pltpu

F32 = jnp.float32
BF = jnp.bfloat16

D_MODEL = 1024
DEPTH = 4
GRID_W = 64
N_MIXERS = 3
ALPHA = (2.0 * DEPTH) ** 0.25
LN_EPS = 1e-5

GMLP_CHUNK = 128
GMLP_WIDTH = 2 * D_MODEL
GMLP_GROUPS = 8
GMLP_GROUP_DIM = GMLP_WIDTH // GMLP_GROUPS

SSD_INNER = 2 * D_MODEL
SSD_HEADDIM = 64
SSD_HEADS = SSD_INNER // SSD_HEADDIM
SSD_GROUPS = 4
SSD_STATE = 128
SSD_CHUNK = 128
SSD_CONV_DIM = SSD_INNER + 2 * SSD_GROUPS * SSD_STATE

HGRN_HEADS = 8
HGRN_KDIM = 128
HGRN_VDIM = D_MODEL // HGRN_HEADS
HGRN_KW = HGRN_HEADS * HGRN_KDIM
HGRN_VW = HGRN_HEADS * HGRN_VDIM
HGRN_CHUNK = 64
HGRN_BLOCK = 8

N_EXPERTS = 16
N_EXPERT_GROUPS = 4
EXPERTS_PER_GROUP = N_EXPERTS // N_EXPERT_GROUPS
TOP_K = 2
D_EXPERT = 256

LANES = 128
BF16_ROWS = 16
VMEM_LIMIT_BYTES = 56 * 1024 * 1024
SQRT_HALF = 0.7071067811865476
LOG2_E = 1.4426950408889634

SSD_XS_CB = SSD_INNER // LANES
SSD_Z_CB0 = SSD_XS_CB
SSD_BC_CB0 = 2 * SSD_XS_CB
SSD_BC_CB = 2 * SSD_GROUPS * SSD_STATE // LANES
SSD_N16 = SSD_BC_CB0 + SSD_BC_CB
SSD_CONV_CB = SSD_XS_CB + SSD_BC_CB
SSD_WIDE_SPLIT = 512
SSD_HALO_PAD = 8

HGRN_N16 = (HGRN_KW + 2 * HGRN_VW) // LANES


def _cparams(n_axes):
    return pltpu.CompilerParams(dimension_semantics=("arbitrary",) * n_axes,
                                vmem_limit_bytes=VMEM_LIMIT_BYTES)


def _resident(shape):
    nd = len(shape)
    return pl.BlockSpec(shape, lambda *_: (0,) * nd, pipeline_mode=pl.Buffered(1))


def _dot(a, b):
    return jnp.dot(a, b, preferred_element_type=F32)


def _dot_nt(a, b):
    return lax.dot_general(a, b, (((1,), (1,)), ((), ())), preferred_element_type=F32)


def _dot_tn(a, b):
    return lax.dot_general(a, b, (((0,), (0,)), ((), ())), preferred_element_type=F32)


def _split3(a):
    a0 = a.astype(BF)
    r1 = a - a0.astype(F32)
    a1 = r1.astype(BF)
    a2 = (r1 - a1.astype(F32)).astype(BF)
    return a0, a1, a2


def _dot_exact_lhs(m_bf, a):
    a0, a1, a2 = _split3(a)
    return _dot(m_bf, a0) + _dot(m_bf, a1) + _dot(m_bf, a2)


def _silu(t):
    return t * jax.nn.sigmoid(t)


def _gelu(t):
    return 0.5 * t * (1.0 + lax.erf(t * SQRT_HALF))


def _softplus(t):
    return jnp.maximum(t, 0.0) + jnp.log1p(jnp.exp(-jnp.abs(t)))


def _layer_norm(v, g, b):
    mu = jnp.mean(v, axis=-1, keepdims=True)
    c = v - mu
    var = jnp.mean(c * c, axis=-1, keepdims=True)
    return c * lax.rsqrt(var + LN_EPS) * g + b


def _mod_index(shared):
    return (lambda b, i: (0, 0, 0)) if shared else (lambda b, i: (b, 0, 0))


MOD_ROWS = 16
MOD_TN = 1536


def _mod_kernel(c_ref, w_ref, b_ref, o_ref):
    c = c_ref[...]
    s = _silu(c).astype(BF)
    o_ref[0] = _dot(s, w_ref[0].astype(BF)) + b_ref[0]


def _modulation(cond, w_mod, b_mod):
    n = 6 * D_MODEL
    return pl.pallas_call(
        _mod_kernel,
        out_shape=jax.ShapeDtypeStruct((DEPTH, MOD_ROWS, n), F32),
        grid=(DEPTH, n // MOD_TN),
        in_specs=[
            pl.BlockSpec((MOD_ROWS, D_MODEL), lambda l, j: (0, 0)),
            pl.BlockSpec((1, D_MODEL, MOD_TN), lambda l, j: (l, 0, j)),
            pl.BlockSpec((1, 1, MOD_TN), lambda l, j: (l, 0, j)),
        ],
        out_specs=pl.BlockSpec((1, MOD_ROWS, MOD_TN), lambda l, j: (l, 0, j)),
        compiler_params=_cparams(2),
        name="modulation",
    )(cond, w_mod, b_mod.reshape(DEPTH, 1, n))


INPROJ_TM = 512
INPROJ_NSPLIT = 8


def _inproj_kernel(x_ref, mod_ref, w_ref, o16_ref, o32_ref, *, n16, n32):
    x = x_ref[0]
    h = (x * (1.0 + mod_ref[0, 1:2, :]) + mod_ref[0, 0:1, :]).astype(BF)
    ncb = n16 + n32
    for c0 in range(0, ncb, INPROJ_NSPLIT):
        n = min(INPROJ_NSPLIT, ncb - c0)
        acc = _dot(h, w_ref[:, c0 * LANES:(c0 + n) * LANES])
        for c in range(n):
            blk = acc[:, c * LANES:(c + 1) * LANES]
            if c0 + c < n16:
                o16_ref[0, c0 + c] = blk.astype(BF)
            else:
                o32_ref[0, c0 + c - n16] = blk


def _inproj(x, mod, w, shared, n16):
    bsz, L, _ = x.shape
    n32 = w.shape[1] // LANES - n16
    tm = min(INPROJ_TM, L)
    return pl.pallas_call(
        functools.partial(_inproj_kernel, n16=n16, n32=n32),
        out_shape=(jax.ShapeDtypeStruct((bsz, n16, L, LANES), BF),
                   jax.ShapeDtypeStruct((bsz, n32, L, LANES), F32)),
        grid=(bsz, L // tm),
        in_specs=[
            pl.BlockSpec((1, tm, D_MODEL), lambda b, i: (b, i, 0)),
            pl.BlockSpec((1, 6, D_MODEL), _mod_index(shared)),
            _resident(w.shape),
        ],
        out_specs=(pl.BlockSpec((1, n16, tm, LANES), lambda b, i: (b, 0, i, 0)),
                   pl.BlockSpec((1, n32, tm, LANES), lambda b, i: (b, 0, i, 0))),
        compiler_params=_cparams(2),
        name="inproj",
    )(x, mod, w)


GMLP_TM = 256
GMLP_NSPLIT = 512


def _gmlp_kernel(*refs, has_pos, tm, n_tiles):
    if has_pos:
        xa_ref, posa_ref, moda_ref, xb_ref, posb_ref, modb_ref, *refs = refs
    else:
        xa_ref, moda_ref, xb_ref, modb_ref, *refs = refs
    (win_ref, vg_ref, vb_ref, ws_ref, bst_ref, wout_ref, lng_ref, lnb_ref,
     o_ref, u_ref, v_ref, gated_ref) = refs
    t = pl.program_id(0)

    def stage_a(slot):
        x = xa_ref[...]
        if has_pos:
            x = x + posa_ref[...]
        h = (x * (1.0 + moda_ref[0, 1:2, :]) + moda_ref[0, 0:1, :]).astype(BF)
        for c0 in range(0, GMLP_WIDTH, GMLP_NSPLIT):
            u_ref[slot, :, c0:c0 + GMLP_NSPLIT] = _gelu(_dot(h, win_ref[:, c0:c0 + GMLP_NSPLIT]))
            v_ref[slot, :, c0:c0 + GMLP_NSPLIT] = _gelu(
                _dot(h, win_ref[:, GMLP_WIDTH + c0:GMLP_WIDTH + c0 + GMLP_NSPLIT]))

    def stage_b(slot):
        x = xb_ref[...]
        if has_pos:
            x = x + posb_ref[...]
        vn = _layer_norm(v_ref[slot], vg_ref[...], vb_ref[...]).astype(BF)
        for c in range(tm // GMLP_CHUNK):
            r0 = c * GMLP_CHUNK
            for g in range(GMLP_GROUPS):
                c0 = g * GMLP_GROUP_DIM
                vv = vn[r0:r0 + GMLP_CHUNK, c0:c0 + GMLP_GROUP_DIM]
                mixed = _dot(ws_ref[g], vv) + bst_ref[:, g:g + 1]
                gated_ref[r0:r0 + GMLP_CHUNK, c0:c0 + GMLP_GROUP_DIM] = (
                    u_ref[slot, r0:r0 + GMLP_CHUNK, c0:c0 + GMLP_GROUP_DIM] * mixed).astype(BF)
        out = _dot(gated_ref[...], wout_ref[...])
        y = ALPHA * x + modb_ref[0, 2:3, :] * out
        o_ref[...] = _layer_norm(y, lng_ref[...], lnb_ref[...])

    @pl.when(t == 0)
    def _():
        stage_a(0)

    @pl.when(jnp.logical_and(t > 0, t < n_tiles))
    def _():
        stage_b((t - 1) % 2)
        stage_a(t % 2)

    @pl.when(t == n_tiles)
    def _():
        stage_b((n_tiles - 1) % 2)


def _gmlp_layer(x, pos, mod, shared, w_in, v_g, v_b, w_s, b_st, w_out, ln_g, ln_b):
    bsz, L, _ = x.shape
    tm = GMLP_TM
    tiles_per_seq = L // tm
    n_tiles = bsz * tiles_per_seq
    has_pos = pos is not None
    tile_a = lambda t: jnp.minimum(t, n_tiles - 1)
    tile_b = lambda t: jnp.maximum(t - 1, 0)
    seq = (lambda tile: 0) if shared else (lambda tile: tile // tiles_per_seq)
    xf = x.reshape(bsz * L, D_MODEL)
    in_specs, args = [], []
    for tile in (tile_a, tile_b):
        in_specs.append(pl.BlockSpec((tm, D_MODEL), lambda t, tile=tile: (tile(t), 0)))
        args.append(xf)
        if has_pos:
            in_specs.append(pl.BlockSpec((tm, D_MODEL), lambda t, tile=tile: (tile(t) % tiles_per_seq, 0)))
            args.append(pos)
        in_specs.append(pl.BlockSpec((1, 6, D_MODEL), lambda t, tile=tile: (seq(tile(t)), 0, 0)))
        args.append(mod)
    for a in (w_in, v_g, v_b, w_s, b_st, w_out, ln_g, ln_b):
        in_specs.append(_resident(a.shape))
        args.append(a)
    return pl.pallas_call(
        functools.partial(_gmlp_kernel, has_pos=has_pos, tm=tm, n_tiles=n_tiles),
        out_shape=jax.ShapeDtypeStruct((bsz * L, D_MODEL), F32),
        grid=(n_tiles + 1,),
        in_specs=in_specs,
        out_specs=pl.BlockSpec((tm, D_MODEL), lambda t: (tile_b(t), 0)),
        scratch_shapes=[pltpu.VMEM((2, tm, GMLP_WIDTH), F32), pltpu.VMEM((2, tm, GMLP_WIDTH), F32),
                        pltpu.VMEM((tm, GMLP_WIDTH), BF)],
        compiler_params=_cparams(1),
        name="gmlp_layer",
    )(*args).reshape(bsz, L, D_MODEL)


def _ssd_scan_kernel(*refs, nc, has_init):
    dirs = []
    for _ in range(2):
        dirs.append(refs[:7])
        refs = refs[7:]
    convw_ref, convb_ref, dtb_ref, alog_ref, dskip_ref, expand_ref, *refs = refs
    if has_init:
        h0_ref, *refs = refs
    yf_ref, yb_ref, st_ref, state_ref, act_ref, acs_ref, acst_ref, wide_ref, cb_ref, halo_ref = refs
    y_refs = (yf_ref, yb_ref)
    q = SSD_CHUNK
    i = pl.program_id(1)

    @pl.when(i == 0)
    def _():
        if has_init:
            state_ref[...] = h0_ref[0, 0]
        else:
            state_ref[...] = jnp.zeros(state_ref.shape, F32)

    row = lax.broadcasted_iota(jnp.int32, (q, q), 0)
    col = lax.broadcasted_iota(jnp.int32, (q, q), 1)
    lo = col < SSD_HEADDIM
    rowlo = row < SSD_HEADDIM
    a_neg = -jnp.exp(alog_ref[...]) * LOG2_E
    tris = [(row >= col), (row <= col)]
    chunk_decays = []

    for d in range(2):
        xs_ref, bc_ref, xsp_ref, bcp_ref, xsn_ref, bcn_ref, dt_ref = dirs[d]
        cidx = i if d == 0 else nc - 1 - i
        first = cidx == 0
        last = cidx == nc - 1
        for c in range(SSD_CONV_CB):
            if c < SSD_XS_CB:
                main, pr, nx = xs_ref[0, c], xsp_ref[0, c], xsn_ref[0, c]
            else:
                cc = c - SSD_XS_CB
                main, pr, nx = bc_ref[0, cc], bcp_ref[0, cc], bcn_ref[0, cc]
            main = main.astype(F32)
            h0 = SSD_HALO_PAD
            halo_ref[d, c, h0:h0 + q] = main
            halo_ref[d, c, h0 - 1:h0] = jnp.where(first, 0.0, pr.astype(F32)[BF16_ROWS - 1:BF16_ROWS, :])
            halo_ref[d, c, h0 + q:h0 + q + 1] = jnp.where(last, 0.0, nx.astype(F32)[0:1, :])
            xp = halo_ref[d, c, h0 - 1:h0 - 1 + q]
            xn = halo_ref[d, c, h0 + 1:h0 + 1 + q]
            w = convw_ref[c]
            act_ref[d, c] = _silu(w[0:1, :] * xp + w[1:2, :] * main + w[2:3, :] * xn + convb_ref[c])

        dt = _softplus(dt_ref[0, 0] + dtb_ref[...])
        acs = _dot_exact_lhs(jnp.where(tris[d], 1.0, 0.0).astype(BF), dt * a_neg)
        acs_ref[d] = acs
        acst_ref[d] = acs.T
        tot = acs[q - 1:q, :] if d == 0 else acs[0:1, :]
        chunk_decays.append(jnp.exp2(tot))
        cols = jnp.concatenate([dt, jnp.exp2(acs), jnp.exp2(tot - acs) * dt], axis=0)
        c_hi = cols.astype(BF)
        c_lo = (cols - c_hi.astype(F32)).astype(BF)
        for c0 in range(0, SSD_INNER, SSD_WIDE_SPLIT):
            e_blk = expand_ref[d, :, c0:c0 + SSD_WIDE_SPLIT]
            wide_ref[d, :, c0:c0 + SSD_WIDE_SPLIT] = _dot(c_hi, e_blk) + _dot(c_lo, e_blk)
        for g in range(SSD_GROUPS):
            cb_ref[d, g] = _dot_nt(act_ref[d,SSD_XS_CB + SSD_GROUPS + g].astype(BF),
                                   act_ref[d,SSD_XS_CB + g].astype(BF))

    for d in range(2):
        tri = tris[d]
        chunk_decay = chunk_decays[d]
        for g in range(SSD_GROUPS):
            b_g = act_ref[d,SSD_XS_CB + g].astype(BF)
            c_g = act_ref[d,SSD_XS_CB + SSD_GROUPS + g].astype(BF)
            cb = cb_ref[d, g]
            for pp in range(SSD_XS_CB // SSD_GROUPS):
                p = g * (SSD_XS_CB // SSD_GROUPS) + pp
                k0 = d * SSD_HEADS + 2 * p
                k1 = k0 + 1
                xblk = act_ref[d,p]
                ps = slice(p * LANES, (p + 1) * LANES)
                ws = []
                for k in (k0, k1):
                    seg = acs_ref[d, :, k:k + 1] - acst_ref[d, k:k + 1, :]
                    ws.append((cb * jnp.exp2(jnp.where(tri, seg, -jnp.inf))).astype(BF))
                xdt = xblk * wide_ref[d, 0:q, ps]
                x_bd = jnp.concatenate([jnp.where(lo, xdt, 0.0).astype(BF),
                                        jnp.where(lo, 0.0, xdt).astype(BF)], axis=0)
                y = _dot(jnp.concatenate(ws, axis=1), x_bd)
                s_in = state_ref[d, p]
                y = y + _dot_nt(c_g, s_in.astype(BF)) * wide_ref[d, q:2 * q, ps]
                if d == 0:
                    y = y + xblk * dskip_ref[p]
                y_refs[d][0, p] = y.astype(BF)
                xw =(xblk * wide_ref[d, 2 * q:3 * q, ps]).astype(BF)
                state_ref[d, p] = s_in * jnp.where(
                    rowlo, chunk_decay[:, k0:k0 + 1], chunk_decay[:, k1:k1 + 1]) + _dot_tn(xw, b_g)

    @pl.when(i == nc - 1)
    def _():
        st_ref[0] = state_ref[...]


def _ssd_expand_matrix():
    k = jnp.arange(LANES)[None, :, None]
    head = (jnp.arange(SSD_INNER) // SSD_HEADDIM)[None, None, :]
    d = jnp.arange(2)[:, None, None]
    return (k == d * SSD_HEADS + head).astype(BF)


def _ssd_scan(proj16, proj32, h0, convw, convb, dtb, alog, dskip):
    bsz, _, L, _ = proj16.shape
    q = SSD_CHUNK
    nc = L // q
    halo = BF16_ROWS
    halo_max = L // halo - 1
    hpc = q // halo

    def specs(cmap):
        prev = lambda b, i: jnp.maximum(cmap(i) * hpc - 1, 0)
        nxt = lambda b, i: jnp.minimum(cmap(i) * hpc + hpc, halo_max)
        bc0 = SSD_BC_CB0 // SSD_BC_CB
        return [
            pl.BlockSpec((1, SSD_XS_CB, q, LANES), lambda b, i: (b, 0, cmap(i), 0)),
            pl.BlockSpec((1, SSD_BC_CB, q, LANES), lambda b, i: (b, bc0, cmap(i), 0)),
            pl.BlockSpec((1, SSD_XS_CB, halo, LANES), lambda b, i: (b, 0, prev(b, i), 0)),
            pl.BlockSpec((1, SSD_BC_CB, halo, LANES), lambda b, i: (b, bc0, prev(b, i), 0)),
            pl.BlockSpec((1, SSD_XS_CB, halo, LANES), lambda b, i: (b, 0, nxt(b, i), 0)),
            pl.BlockSpec((1, SSD_BC_CB, halo, LANES), lambda b, i: (b, bc0, nxt(b, i), 0)),
            pl.BlockSpec((1, 1, q, LANES), lambda b, i: (b, 0, cmap(i), 0)),
        ]

    in_specs = specs(lambda i: i) + specs(lambda i: nc - 1 - i)
    args = ([proj16] * 6 + [proj32]) * 2
    for a in (convw, convb, dtb, alog, dskip, _ssd_expand_matrix()):
        in_specs.append(_resident(a.shape))
        args.append(a)
    has_init = h0 is not None
    st_block = (2, SSD_XS_CB, LANES, SSD_STATE)
    if has_init:
        in_specs.append(pl.BlockSpec((1, 1) + st_block, lambda b, i: (b, 0, 0, 0, 0, 0)))
        args.append(h0)
    y_shape = jax.ShapeDtypeStruct((bsz, SSD_XS_CB, L, LANES), BF)
    return pl.pallas_call(
        functools.partial(_ssd_scan_kernel, nc=nc, has_init=has_init),
        out_shape=(y_shape, y_shape, jax.ShapeDtypeStruct((bsz,) + st_block, F32)),
        grid=(bsz, nc),
        in_specs=in_specs,
        out_specs=(
            pl.BlockSpec((1, SSD_XS_CB, q, LANES), lambda b, i: (b, 0, i, 0)),
            pl.BlockSpec((1, SSD_XS_CB, q, LANES), lambda b, i: (b, 0, nc - 1 - i, 0)),
            pl.BlockSpec((1,) + st_block, lambda b, i: (b, 0, 0, 0, 0)),
        ),
        scratch_shapes=[pltpu.VMEM(st_block, F32), pltpu.VMEM((2, SSD_CONV_CB, q, LANES), F32),
                        pltpu.VMEM((2, q, LANES), F32), pltpu.VMEM((2, q, LANES), F32),
                        pltpu.VMEM((2, 3 * q, SSD_INNER), F32),
                        pltpu.VMEM((2, SSD_GROUPS, q, q), F32),
                        pltpu.VMEM((2, SSD_CONV_CB, q + 2 * SSD_HALO_PAD, LANES), F32)],
        compiler_params=_cparams(2),
        name="ssd_scan",
    )(*args)


OUT_TM = 256


def _ssd_out_kernel(yf_ref, yb_ref, z_ref, x_ref, mod_ref, ng_ref, wout_ref, lng_ref, lnb_ref,
                    o_ref, yz_ref, buf_ref):
    ssq = None
    for c in range(SSD_XS_CB):
        yz = (yf_ref[0, c].astype(F32) + yb_ref[0, c].astype(F32)) * _silu(z_ref[0, c].astype(F32))
        yz_ref[c] = yz
        ssq = yz * yz if ssq is None else ssq + yz * yz
    r = lax.rsqrt(jnp.sum(ssq, axis=-1, keepdims=True) * (1.0 / SSD_INNER) + LN_EPS)
    for c in range(SSD_XS_CB):
        buf_ref[:, c * LANES:(c + 1) * LANES] = (yz_ref[c] * r * ng_ref[c]).astype(BF)
    out = _dot(buf_ref[...], wout_ref[...])
    y = ALPHA * x_ref[0] + mod_ref[0, 2:3, :] * out
    o_ref[0] = _layer_norm(y, lng_ref[...], lnb_ref[...])


def _ssd_out(yf, yb, proj, x, mod, shared, norm_g, w_out, ln_g, ln_b):
    bsz, L, _ = x.shape
    tm = OUT_TM
    blk = pl.BlockSpec((1, SSD_XS_CB, tm, LANES), lambda b, i: (b, 0, i, 0))
    return pl.pallas_call(
        _ssd_out_kernel,
        out_shape=jax.ShapeDtypeStruct((bsz, L, D_MODEL), F32),
        grid=(bsz, L // tm),
        in_specs=[
            blk, blk,
            pl.BlockSpec((1, SSD_XS_CB, tm, LANES), lambda b, i: (b, SSD_Z_CB0 // SSD_XS_CB, i, 0)),
            pl.BlockSpec((1, tm, D_MODEL), lambda b, i: (b, i, 0)),
            pl.BlockSpec((1, 6, D_MODEL), _mod_index(shared)),
            _resident(norm_g.shape), _resident(w_out.shape),
            _resident(ln_g.shape), _resident(ln_b.shape),
        ],
        out_specs=pl.BlockSpec((1, tm, D_MODEL), lambda b, i: (b, i, 0)),
        scratch_shapes=[pltpu.VMEM((SSD_XS_CB, tm, LANES), F32), pltpu.VMEM((tm, SSD_INNER), BF)],
        compiler_params=_cparams(2),
        name="ssd_out",
    )(yf, yb, proj, x, mod, norm_g, w_out, ln_g, ln_b)


def _hgrn_scan_kernel(*refs, nc, layer, has_init):
    dirs = (refs[0:3], refs[3:6])
    lbl_ref, *refs = refs[6:]
    if has_init:
        h0_ref, *refs = refs
    of_ref, ob_ref, st_ref, state_ref, kk_ref, bcs_ref, roff_ref, rdiag_ref, ointer_ref = refs
    o_refs = (of_ref, ob_ref)
    q = HGRN_CHUNK
    blk = HGRN_BLOCK
    nb = q // blk
    i = pl.program_id(1)

    @pl.when(i == 0)
    def _():
        for d in range(2):
            for h in range(HGRN_HEADS):
                if has_init:
                    state_ref[d, h] = h0_ref[0, 0, d, h].T
                else:
                    state_ref[d, h] = jnp.zeros((HGRN_VDIM, HGRN_KDIM), F32)

    row = lax.broadcasted_iota(jnp.int32, (q, q), 0)
    col = lax.broadcasted_iota(jnp.int32, (q, q), 1)
    lane = lax.broadcasted_iota(jnp.int32, (blk, q), 1)
    lane_blk = lane // blk
    ones_bf = jnp.ones((HGRN_KDIM, q), BF)

    tris = [(row >= col), (row <= col)]
    all_pairs = [[(b, b2) for b in range(nb) for b2 in range(nb) if (b2 < b if d == 0 else b2 > b)]
                 for d in range(2)]

    for d in range(2):
        q_ref, f_ref, v_ref = dirs[d]
        tri = tris[d]
        logits = [lbl_ref[j, d] for j in range(DEPTH)]
        mx = functools.reduce(jnp.maximum, logits)
        es = [jnp.exp(t - mx) for t in logits]
        den = functools.reduce(lambda a, b: a + b, es)
        lb = jnp.zeros_like(den)
        for j in range(1, layer + 1):
            lb = lb + es[j] / den
        gs = []
        for h in range(HGRN_HEADS):
            hs = slice(h * HGRN_KDIM, (h + 1) * HGRN_KDIM)
            f = lb[:, hs] + (1.0 - lb[:, hs]) * jax.nn.sigmoid(f_ref[0, h])
            kk_ref[d, :, hs] = 1.0 - f
            gs.append(jnp.log(f) * LOG2_E)
        bcs_ref[d] = _dot_exact_lhs(jnp.where(tri, 1.0, 0.0).astype(BF), jnp.concatenate(gs, axis=1))

    for d in range(2):
        q_ref, f_ref, v_ref = dirs[d]
        end_row = (lambda b: b * blk + blk - 1) if d == 0 else (lambda b: b * blk)
        tot_row = q - 1 if d == 0 else 0
        pairs = all_pairs[d]
        for h in range(HGRN_HEADS):
            hs = slice(h * HGRN_KDIM, (h + 1) * HGRN_KDIM)
            qh = q_ref[0, h].astype(F32)
            vh = v_ref[0, h]
            kk = kk_ref[d, :, hs]
            bcs = bcs_ref[d, :, hs]
            s_t = state_ref[d, h]
            ointer_ref[d, h] = _dot_nt((qh * jnp.exp2(bcs)).astype(BF), s_t.astype(BF))
            tot = bcs_ref[d, tot_row:tot_row + 1, hs]
            kdec = kk * jnp.exp2(tot - bcs)
            state_ref[d, h] = s_t * jnp.exp2(tot) + _dot_tn(vh, kdec.astype(BF))
            r_end = [bcs_ref[d, end_row(b):end_row(b) + 1, hs] for b in range(nb)]
            kend = jnp.concatenate(
                [kk[b * blk:(b + 1) * blk] * jnp.exp2(r_end[b] - bcs[b * blk:(b + 1) * blk])
                 for b in range(nb)], axis=0)
            q_pairs = jnp.concatenate(
                [qh[b * blk:(b + 1) * blk] * jnp.exp2(bcs[b * blk:(b + 1) * blk] - r_end[b2])
                 for b, b2 in pairs], axis=0)
            roff_ref[d, h] = _dot_nt(q_pairs.astype(BF), kend.astype(BF))
            p_rows = []
            for b in range(nb):
                bq = bcs[b * blk:(b + 1) * blk]
                qb = qh[b * blk:(b + 1) * blk]
                for j in range(blk):
                    s = b * blk + j
                    p_rows.append(qb * (kk_ref[d, s:s + 1, hs] * jnp.exp2(bq - bcs_ref[d, s:s + 1, hs])))
            rdiag_ref[d, h] = _dot(jnp.concatenate(p_rows, axis=0).astype(BF), ones_bf)

    for d in range(2):
        v_ref = dirs[d][2]
        pairs = all_pairs[d]
        for h in range(HGRN_HEADS):
            blocks = []
            for b in range(nb):
                att_b = jnp.zeros((blk, q), F32)
                for n, (pb, b2) in enumerate(pairs):
                    if pb == b:
                        att_b = jnp.where(lane_blk == b2, roff_ref[d, h, n * blk:(n + 1) * blk], att_b)
                for j in range(blk):
                    s = b * blk + j
                    att_b = jnp.where(lane == s, rdiag_ref[d, h, s * blk:(s + 1) * blk], att_b)
                blocks.append(att_b)
            att = jnp.where(tris[d], jnp.concatenate(blocks, axis=0), 0.0)
            o_refs[d][0, h] = (ointer_ref[d, h] + _dot(att.astype(BF), v_ref[0, h])).astype(BF)

    @pl.when(i == nc - 1)
    def _():
        for d in range(2):
            for h in range(HGRN_HEADS):
                st_ref[0, d, h] = state_ref[d, h].T


def _hgrn_scan(proj16, proj32, h0, lb_logits, layer):
    bsz, _, L, _ = proj16.shape
    q = HGRN_CHUNK
    nc = L // q
    nh = HGRN_HEADS

    def specs(cmap, d):
        return [
            pl.BlockSpec((1, nh, q, LANES), lambda b, i: (b, 0, cmap(i), 0)),
            pl.BlockSpec((1, nh, q, LANES), lambda b, i: (b, d, cmap(i), 0)),
            pl.BlockSpec((1, nh, q, LANES), lambda b, i: (b, 1, cmap(i), 0)),
        ]

    in_specs = specs(lambda i: i, 0) + specs(lambda i: nc - 1 - i, 1) + [_resident(lb_logits.shape)]
    args = [proj16, proj32, proj16] * 2 + [lb_logits]
    has_init = h0 is not None
    st_block = (2, nh, HGRN_KDIM, HGRN_VDIM)
    if has_init:
        in_specs.append(pl.BlockSpec((1, 1) + st_block, lambda b, i: (b, 0, 0, 0, 0, 0)))
        args.append(h0)
    o_shape = jax.ShapeDtypeStruct((bsz, nh, L, LANES), BF)
    n_blocks = q // HGRN_BLOCK
    n_pairs = n_blocks * (n_blocks - 1) // 2
    return pl.pallas_call(
        functools.partial(_hgrn_scan_kernel, nc=nc, layer=layer, has_init=has_init),
        out_shape=(o_shape, o_shape, jax.ShapeDtypeStruct((bsz,) + st_block, F32)),
        grid=(bsz, nc),
        in_specs=in_specs,
        out_specs=(
            pl.BlockSpec((1, nh, q, LANES), lambda b, i: (b, 0, i, 0)),
            pl.BlockSpec((1, nh, q, LANES), lambda b, i: (b, 0, nc - 1 - i, 0)),
            pl.BlockSpec((1,) + st_block, lambda b, i: (b, 0, 0, 0, 0)),
        ),
        scratch_shapes=[pltpu.VMEM((2, nh, HGRN_VDIM, HGRN_KDIM), F32),
                        pltpu.VMEM((2, q, HGRN_KW), F32), pltpu.VMEM((2, q, HGRN_KW), F32),
                        pltpu.VMEM((2, nh, n_pairs * HGRN_BLOCK, q), F32),
                        pltpu.VMEM((2, nh, q * HGRN_BLOCK, q), F32),
                        pltpu.VMEM((2, nh, q, HGRN_VDIM), F32)],
        compiler_params=_cparams(2),
        name="hgrn_scan",
    )(*args)


def _hgrn_out_kernel(of_ref, ob_ref, g_ref, x_ref, mod_ref, ng_ref, wout_ref, lng_ref, lnb_ref,
                     o_ref, buf_ref):
    for c in range(HGRN_HEADS):
        o = of_ref[0, c].astype(F32) + ob_ref[0, c].astype(F32)
        r = lax.rsqrt(jnp.mean(o * o, axis=-1, keepdims=True) + LN_EPS)
        buf_ref[:, c * LANES:(c + 1) * LANES] = (
            o * r * ng_ref[...] * _silu(g_ref[0, c].astype(F32))).astype(BF)
    out = _dot(buf_ref[...], wout_ref[...])
    y = ALPHA * x_ref[0] + mod_ref[0, 2:3, :] * out
    o_ref[0] = _layer_norm(y, lng_ref[...], lnb_ref[...])


def _hgrn_out(of, ob, proj, x, mod, shared, norm_g, w_out, ln_g, ln_b):
    bsz, L, _ = x.shape
    tm = OUT_TM
    nh = HGRN_HEADS
    blk = pl.BlockSpec((1, nh, tm, LANES), lambda b, i: (b, 0, i, 0))
    return pl.pallas_call(
        _hgrn_out_kernel,
        out_shape=jax.ShapeDtypeStruct((bsz, L, D_MODEL), F32),
        grid=(bsz, L // tm),
        in_specs=[
            blk, blk,
            pl.BlockSpec((1, nh, tm, LANES), lambda b, i: (b, 2, i, 0)),
            pl.BlockSpec((1, tm, D_MODEL), lambda b, i: (b, i, 0)),
            pl.BlockSpec((1, 6, D_MODEL), _mod_index(shared)),
            _resident(norm_g.shape), _resident(w_out.shape),
            _resident(ln_g.shape), _resident(ln_b.shape),
        ],
        out_specs=pl.BlockSpec((1, tm, D_MODEL), lambda b, i: (b, i, 0)),
        scratch_shapes=[pltpu.VMEM((tm, HGRN_VW), BF)],
        compiler_params=_cparams(2),
        name="hgrn_out",
    )(of, ob, proj, x, mod, norm_g, w_out, ln_g, ln_b)


MOE_TM = 512
MOE_DOWN_SPLIT = 4


def _route(scores, sel):
    s = [scores[e:e + 1] for e in range(N_EXPERTS)]
    v = [sel[e:e + 1] for e in range(N_EXPERTS)]
    top2 = []
    for e in range(N_EXPERTS):
        g0 = (e // EXPERTS_PER_GROUP) * EXPERTS_PER_GROUP
        rank = None
        for j in range(g0, g0 + EXPERTS_PER_GROUP):
            if j == e:
                continue
            beats = jnp.where((v[j] >= v[e]) if j < e else (v[j] > v[e]), 1.0, 0.0)
            rank = beats if rank is None else rank + beats
        top2.append(rank < TOP_K)
    gscore = []
    for g in range(N_EXPERT_GROUPS):
        acc = None
        for e in range(g * EXPERTS_PER_GROUP, (g + 1) * EXPERTS_PER_GROUP):
            t = jnp.where(top2[e], v[e], 0.0)
            acc = t if acc is None else acc + t
        gscore.append(acc)
    best = []
    for g in range(N_EXPERT_GROUPS):
        ok = None
        for g2 in range(N_EXPERT_GROUPS):
            if g2 == g:
                continue
            c = (gscore[g2] < gscore[g]) if g2 < g else (gscore[g2] <= gscore[g])
            ok = c if ok is None else jnp.logical_and(ok, c)
        best.append(ok)
    chosen = [jnp.logical_and(best[e // EXPERTS_PER_GROUP], top2[e]) for e in range(N_EXPERTS)]
    den = None
    for e in range(N_EXPERTS):
        t = jnp.where(chosen[e], s[e], 0.0)
        den = t if den is None else den + t
    return jnp.concatenate([jnp.where(chosen[e], s[e] / den, 0.0) for e in range(N_EXPERTS)], axis=0)


def _moe_kernel(xa_ref, moda_ref, xb_ref, modb_ref, wrh_ref, wrl_ref, br_ref, wgu_ref, wd_ref,
                lng_ref, lnb_ref, o_ref, hid_ref, *, n_tiles):
    t = pl.program_id(0)

    def step(do_a, do_b, slot_a, slot_b):
        if do_b:
            hid_b = hid_ref[slot_b]
            w = D_MODEL // MOE_DOWN_SPLIT
            outs = [_dot(hid_b, wd_ref[:, 0:w])]
        if do_a:
            h = xa_ref[...] * (1.0 + moda_ref[0, 4:5, :]) + moda_ref[0, 3:4, :]
            h_hi = h.astype(BF)
            h_lo = (h - h_hi.astype(F32)).astype(BF)
            logits = (_dot_nt(wrh_ref[...], h_hi) + _dot_nt(wrl_ref[...], h_hi)
                      + _dot_nt(wrh_ref[...], h_lo))
        if do_b:
            outs.append(_dot(hid_b, wd_ref[:, w:2 * w]))
        if do_a:
            scores = jax.nn.sigmoid(logits)
            cw = _route(scores, scores + br_ref[...]).T
        if do_b:
            for k in range(2, MOE_DOWN_SPLIT):
                outs.append(_dot(hid_b, wd_ref[:, k * w:(k + 1) * w]))

        def expert(e):
            gu = _dot(h_hi, wgu_ref[e])
            hid = _silu(gu[:, :D_EXPERT]) * gu[:, D_EXPERT:] * cw[:, e:e + 1]
            hid_ref[slot_a, :, e * D_EXPERT:(e + 1) * D_EXPERT] = hid.astype(BF)

        if do_a:
            expert(0)
        if do_b:
            y = ALPHA * xb_ref[...] + modb_ref[0, 5:6, :] * jnp.concatenate(outs, axis=1)
            o_ref[...] = _layer_norm(y, lng_ref[...], lnb_ref[...])
        if do_a:
            for e in range(1, N_EXPERTS):
                expert(e)

    @pl.when(t == 0)
    def _():
        step(True, False, 0, None)

    @pl.when(jnp.logical_and(t > 0, t < n_tiles))
    def _():
        step(True, True, t % 2, (t - 1) % 2)

    @pl.when(t == n_tiles)
    def _():
        step(False, True, None, (n_tiles - 1) % 2)


def _moe_layer(x, mod, shared, wr_hi, wr_lo, b_r, w_gu, w_down, ln_g, ln_b):
    bsz, L, _ = x.shape
    tm = MOE_TM
    tiles_per_seq = L // tm
    n_tiles = bsz * tiles_per_seq
    w_down = w_down.reshape(N_EXPERTS * D_EXPERT, D_MODEL)
    tile_a = lambda t: jnp.minimum(t, n_tiles - 1)
    tile_b = lambda t: jnp.maximum(t - 1, 0)
    seq = (lambda tile: 0) if shared else (lambda tile: tile // tiles_per_seq)
    return pl.pallas_call(
        functools.partial(_moe_kernel, n_tiles=n_tiles),
        out_shape=jax.ShapeDtypeStruct((bsz * L, D_MODEL), F32),
        grid=(n_tiles + 1,),
        in_specs=[
            pl.BlockSpec((tm, D_MODEL), lambda t: (tile_a(t), 0)),
            pl.BlockSpec((1, 6, D_MODEL), lambda t: (seq(tile_a(t)), 0, 0)),
            pl.BlockSpec((tm, D_MODEL), lambda t: (tile_b(t), 0)),
            pl.BlockSpec((1, 6, D_MODEL), lambda t: (seq(tile_b(t)), 0, 0)),
            _resident(wr_hi.shape), _resident(wr_lo.shape), _resident(b_r.shape),
            _resident(w_gu.shape), _resident(w_down.shape),
            _resident(ln_g.shape), _resident(ln_b.shape),
        ],
        out_specs=pl.BlockSpec((tm, D_MODEL), lambda t: (tile_b(t), 0)),
        scratch_shapes=[pltpu.VMEM((2, tm, N_EXPERTS * D_EXPERT), BF)],
        compiler_params=_cparams(1),
        name="moe_layer",
    )(x.reshape(bsz * L, D_MODEL), mod, x.reshape(bsz * L, D_MODEL), mod,
      wr_hi, wr_lo, b_r, w_gu, w_down, ln_g, ln_b).reshape(bsz, L, D_MODEL)


def _grid_pos_embed(rows):
    quarter = D_MODEL // 4
    freq = jnp.exp(-math.log(10000.0) / quarter * jnp.arange(quarter, dtype=F32))
    r = jnp.repeat(jnp.arange(rows, dtype=F32), GRID_W)
    col = jnp.tile(jnp.arange(GRID_W, dtype=F32), rows)
    ar = r[:, None] * freq[None, :]
    ac = col[:, None] * freq[None, :]
    return jnp.concatenate([jnp.sin(ar), jnp.cos(ar), jnp.sin(ac), jnp.cos(ac)], axis=-1)


def _row(a):
    return a.reshape(1, -1)


def _col_blocks(a):
    n = a.shape[-1] // LANES
    return jnp.moveaxis(a.reshape(a.shape[:-1] + (n, LANES)), -2, 0)


def _prep_weights(p):
    w = {}
    w["gmlp_w_in"] = p["gmlp_w_in"].astype(BF)
    w["gmlp_w_s"] = p["gmlp_w_s"].astype(BF)
    w["gmlp_b_st"] = jnp.swapaxes(p["gmlp_b_s"], 1, 2)
    w["gmlp_w_out"] = p["gmlp_w_out"].astype(BF)
    wi = p["ssd_w_in"]
    n_l = wi.shape[0]
    pad = jnp.zeros((n_l, D_MODEL, LANES - 2 * SSD_HEADS), F32)
    w["ssd_w_in"] = jnp.concatenate([
        wi[:, :, SSD_INNER:2 * SSD_INNER],
        wi[:, :, :SSD_INNER],
        wi[:, :, 2 * SSD_INNER:SSD_INNER + SSD_CONV_DIM],
        wi[:, :, SSD_INNER + SSD_CONV_DIM:], pad], axis=-1).astype(BF)
    w["ssd_conv_w"] = jnp.stack([_col_blocks(t) for t in p["ssd_conv_w"]])
    w["ssd_conv_b"] = jnp.stack([_col_blocks(_row(t)) for t in p["ssd_conv_b"]])
    zpad = jnp.zeros((n_l, LANES - 2 * SSD_HEADS), F32)
    w["ssd_dt_bias"] = jnp.concatenate([p["ssd_dt_bias"].reshape(n_l, -1), zpad], axis=-1)[:, None, :]
    w["ssd_a_log"] = jnp.concatenate([p["ssd_a_log"].reshape(n_l, -1), zpad], axis=-1)[:, None, :]
    w["ssd_d_skip"] = jnp.stack([_col_blocks(_row(jnp.repeat(t, SSD_HEADDIM))) for t in p["ssd_d_skip"]])
    w["ssd_norm_g"] = jnp.stack([_col_blocks(_row(t)) for t in p["ssd_norm_g"]])
    w["ssd_w_out"] = p["ssd_w_out"].astype(BF)
    hw = p["hgrn_w_in"]
    w["hgrn_w_in"] = jnp.concatenate([
        hw[:, :, :HGRN_KW],
        hw[:, :, 3 * HGRN_KW:],
        hw[:, :, HGRN_KW:3 * HGRN_KW]], axis=-1).astype(BF)
    w["hgrn_lb_logits"] = p["hgrn_lb_logits"].reshape(DEPTH, 2, 1, HGRN_KW)
    w["hgrn_w_out"] = p["hgrn_w_out"].astype(BF)
    wr_t = p["moe_w_router"].T
    wr_hi = wr_t.astype(BF)
    w["moe_wr_hi"] = wr_hi
    w["moe_wr_lo"] = (wr_t - wr_hi.astype(F32)).astype(BF)
    w["moe_b_r"] = p["moe_b_router"].reshape(N_EXPERTS, 1)
    w["moe_w_gu"] = p["moe_w_gu"].astype(BF)
    w["moe_w_down"] = p["moe_w_down"].astype(BF)
    return w


def _run_trunk(x, pos, mods, shared, ssd_init, hgrn_init, p, w):
    bsz, L, _ = x.shape
    flat = (1, bsz * L, D_MODEL) if shared else x.shape
    ssd_states, hgrn_states = [], []
    ia = ib = ic = 0
    for li in range(DEPTH):
        mod = mods[li]
        ln_g1, ln_b1 = _row(p["ln_g"][li, 0]), _row(p["ln_b"][li, 0])
        ln_g2, ln_b2 = _row(p["ln_g"][li, 1]), _row(p["ln_b"][li, 1])
        kind = li % N_MIXERS
        if kind == 0:
            x = _gmlp_layer(x.reshape(flat), pos if li == 0 else None, mod, shared,
                            w["gmlp_w_in"][ia], _row(p["gmlp_ln_g"][ia]), _row(p["gmlp_ln_b"][ia]),
                            w["gmlp_w_s"][ia], w["gmlp_b_st"][ia], w["gmlp_w_out"][ia],
                            ln_g1, ln_b1).reshape(bsz, L, D_MODEL)
            ia += 1
        elif kind == 1:
            proj16, proj32 = _inproj(x, mod, w["ssd_w_in"][ib], shared, SSD_N16)
            h0 = None if ssd_init is None else ssd_init.reshape(
                bsz, -1, 2, SSD_XS_CB, LANES, SSD_STATE)[:, ib:ib + 1]
            yf, yb, st = _ssd_scan(proj16, proj32, h0, w["ssd_conv_w"][ib], w["ssd_conv_b"][ib],
                                   w["ssd_dt_bias"][ib], w["ssd_a_log"][ib], w["ssd_d_skip"][ib])
            ssd_states.append(st.reshape(bsz, 2, SSD_HEADS, SSD_HEADDIM, SSD_STATE))
            x = _ssd_out(yf, yb, proj16, x, mod, shared, w["ssd_norm_g"][ib], w["ssd_w_out"][ib],
                         ln_g1, ln_b1)
            ib += 1
        else:
            proj16, proj32 = _inproj(x, mod, w["hgrn_w_in"][ic], shared, HGRN_N16)
            h0 = None if hgrn_init is None else hgrn_init[:, ic:ic + 1]
            of, ob, st = _hgrn_scan(proj16, proj32, h0, w["hgrn_lb_logits"], li)
            hgrn_states.append(st)
            x = _hgrn_out(of, ob, proj16, x, mod, shared, _row(p["hgrn_norm_g"][ic]),
                          w["hgrn_w_out"][ic], ln_g1, ln_b1)
            ic += 1
        x = _moe_layer(x.reshape(flat), mod, shared, w["moe_wr_hi"], w["moe_wr_lo"], w["moe_b_r"],
                       w["moe_w_gu"][li], w["moe_w_down"][li], ln_g2, ln_b2).reshape(bsz, L, D_MODEL)
    return x, jnp.stack(ssd_states, axis=1), jnp.stack(hgrn_states, axis=1)


def kernel(x_prompt, x_sample, state_ssd, state_hgrn, c, c_ctx, w_mod, b_mod, ln_g, ln_b, gmlp_w_in, gmlp_ln_g, gmlp_ln_b, gmlp_w_s, gmlp_b_s, gmlp_w_out, ssd_w_in, ssd_conv_w, ssd_conv_b, ssd_dt_bias, ssd_a_log, ssd_d_skip, ssd_norm_g, ssd_w_out, hgrn_w_in, hgrn_lb_logits, hgrn_norm_g, hgrn_w_out, moe_w_router, moe_b_router, moe_w_gu, moe_w_down):
    p = dict(ln_g=ln_g, ln_b=ln_b, gmlp_w_in=gmlp_w_in, gmlp_ln_g=gmlp_ln_g, gmlp_ln_b=gmlp_ln_b,
             gmlp_w_s=gmlp_w_s, gmlp_b_s=gmlp_b_s, gmlp_w_out=gmlp_w_out, ssd_w_in=ssd_w_in,
             ssd_conv_w=ssd_conv_w, ssd_conv_b=ssd_conv_b, ssd_dt_bias=ssd_dt_bias,
             ssd_a_log=ssd_a_log, ssd_d_skip=ssd_d_skip, ssd_norm_g=ssd_norm_g, ssd_w_out=ssd_w_out,
             hgrn_w_in=hgrn_w_in, hgrn_lb_logits=hgrn_lb_logits, hgrn_norm_g=hgrn_norm_g,
             hgrn_w_out=hgrn_w_out, moe_w_router=moe_w_router, moe_b_router=moe_b_router,
             moe_w_gu=moe_w_gu, moe_w_down=moe_w_down)
    w = _prep_weights(p)
    n_lat = c.shape[0]
    cond = jnp.concatenate([c, c_ctx[None, :],
                            jnp.zeros((MOD_ROWS - n_lat - 1, D_MODEL), F32)], axis=0)
    mods = _modulation(cond, w_mod, b_mod).reshape(DEPTH, MOD_ROWS, 6, D_MODEL)
    mods_lat = [mods[li, :n_lat] for li in range(DEPTH)]
    mods_ctx = [mods[li, n_lat:n_lat + 1] for li in range(DEPTH)]
    y_prompt, new_ssd, new_hgrn = _run_trunk(x_prompt, None, mods_ctx, True, None, None, p, w)
    pos = _grid_pos_embed(x_sample.shape[1] // GRID_W)
    y_sample, _, _ = _run_trunk(x_sample, pos, mods_lat, False, state_ssd, state_hgrn, p, w)
    return (y_prompt, y_sample, new_ssd, new_hgrn)
```
